```python
import jax, jax.numpy as jnp
from jax import lax
import numpy as np

D_MODEL = 1024
BATCH = 8
SEQ = 4096
DEPTH = 2

N_META = 16
D_CONV = 512
CONV_A_WIDTH = 3
DN_HEADS = 4
DN_HEAD_DIM = 128
DN_DIM = DN_HEADS * DN_HEAD_DIM
DN_CONV_WIDTH = 4
DN_CHUNK = 64
D_MIX = D_CONV + DN_DIM
IN_DIM = 3 * D_CONV + 4 * DN_DIM + 2 * DN_HEADS
SWA_HEADS = 16
SWA_KV_HEADS = 4
SWA_HEAD_DIM = 64
SWA_WINDOW = 128
SWA_BLOCK = 128
D_FF = 2816
FFN_CONV_WIDTH = 3
EPS = 1e-6
N_EVEN = (DEPTH + 1) // 2
N_ODD = DEPTH // 2

kernel_name = "hybrid_shortconv_gdn_swa_convffn_meta"


def rms_norm(x, w):
    xf = x.astype(jnp.float32)
    y = xf * lax.rsqrt(jnp.mean(xf * xf, -1, keepdims=True) + EPS)
    return (y * w.astype(jnp.float32)).astype(x.dtype)


def l2_norm(x):
    return x * lax.rsqrt(jnp.sum(x * x, -1, keepdims=True) + EPS)


def causal_dwconv(x, w):
    k = w.shape[0]
    return lax.conv_general_dilated(
        x, w[:, None, :].astype(x.dtype), window_strides=(1,), padding=((k - 1, 0),),
        dimension_numbers=('NWC', 'WIO', 'NWC'), feature_group_count=x.shape[-1])


def gated_delta_rule(q, k, v, beta, g):
    q, k, v, beta, g = (t.astype(jnp.float32) for t in (q, k, v, beta, g))
    b, l, h, dk = q.shape
    dv = v.shape[-1]
    c = DN_CHUNK
    n = l // c

    def chunks(t):
        t = t.reshape((b, n, c, h) + t.shape[3:])
        return jnp.moveaxis(t, 3, 1)

    q, k, v, beta, g = chunks(q), chunks(k), chunks(v), chunks(beta), chunks(g)
    decay = jnp.cumsum(g, -1)
    diff = decay[..., :, None] - decay[..., None, :]
    idx = jnp.arange(c)
    strict = idx[:, None] > idx[None, :]
    incl = idx[:, None] >= idx[None, :]
    dmask = jnp.exp(jnp.where(incl, diff, -jnp.inf))
    kk = jnp.einsum('bhnid,bhnjd->bhnij', k, k)
    a_strict = jnp.where(strict, beta[..., None] * kk * dmask, 0.0)
    t_mat = a_strict + jnp.eye(c, dtype=jnp.float32)
    rhs = jnp.concatenate([v * beta[..., None], k * (beta * jnp.exp(decay))[..., None]], -1)
    sol = lax.linalg.triangular_solve(t_mat, rhs, left_side=True, lower=True, unit_diagonal=True)
    u = sol[..., :dv]
    w = sol[..., dv:]
    qk = jnp.einsum('bhnid,bhnjd->bhnij', q, k) * dmask
    q_dec = q * jnp.exp(decay)[..., None]
    k_dec = k * jnp.exp(decay[..., -1:] - decay)[..., None]
    g_last = jnp.exp(decay[..., -1])

    def step(s, inp):
        u_n, w_n, qk_n, qd_n, kd_n, gl_n = inp
        v_new = u_n - jnp.einsum('bhcd,bhde->bhce', w_n, s)
        o = jnp.einsum('bhcd,bhde->bhce', qd_n, s) + jnp.einsum('bhij,bhje->bhie', qk_n, v_new)
        s = s * gl_n[..., None, None] + jnp.einsum('bhcd,bhce->bhde', kd_n, v_new)
        return s, o

    xs = tuple(jnp.moveaxis(t, 2, 0) for t in (u, w, qk, q_dec, k_dec, g_last))
    s0 = jnp.zeros((b, h, dk, dv), jnp.float32)
    _, o = lax.scan(step, s0, xs)
    return jnp.transpose(o, (1, 0, 3, 2, 4)).reshape(b, l, h, dv)


def even_mixer(h, w_in, conv_a_w, dn_conv_w, a_log, dt_bias, dn_norm_w, w_out):
    b, l, _ = h.shape
    p = h @ w_in
    sizes = [D_CONV, D_CONV, D_CONV, 3 * DN_DIM, DN_DIM, DN_HEADS, DN_HEADS]
    splits = [sum(sizes[:i + 1]) for i in range(len(sizes) - 1)]
    a_gate_in, a_gate_out, a_h, qkv, z, beta_raw, alpha_raw = jnp.split(p, splits, -1)
    y_a = a_gate_out * causal_dwconv(a_gate_in * a_h, conv_a_w)
    qkv = jax.nn.silu(causal_dwconv(qkv, dn_conv_w))
    q, k, v = jnp.split(qkv, 3, -1)
    hd = (b, l, DN_HEADS, DN_HEAD_DIM)
    q = l2_norm(q.reshape(hd).astype(jnp.float32)) * (DN_HEAD_DIM ** -0.5)
    k = l2_norm(k.reshape(hd).astype(jnp.float32))
    v = v.reshape(hd).astype(jnp.float32)
    beta = jax.nn.sigmoid(beta_raw.astype(jnp.float32))
    g = -jnp.exp(a_log.astype(jnp.float32)) * jax.nn.softplus(
        alpha_raw.astype(jnp.float32) + dt_bias.astype(jnp.float32))
    pad = (-N_META) % DN_CHUNK
    padw = lambda t: jnp.pad(t, ((0, 0), (pad, 0)) + ((0, 0),) * (t.ndim - 2))
    o = gated_delta_rule(padw(q), padw(k), padw(v), padw(beta), padw(g))[:, pad:]
    o = rms_norm(o, dn_norm_w) * jax.nn.silu(z.reshape(hd).astype(jnp.float32))
    y_b = o.reshape(b, l, DN_DIM).astype(h.dtype)
    return jnp.concatenate([y_a, y_b], -1) @ w_out


def sink_softmax(logits, sink):
    m = jnp.maximum(jnp.max(logits, -1, keepdims=True), sink)
    e = jnp.exp(logits - m)
    return e / (jnp.sum(e, -1, keepdims=True) + jnp.exp(sink - m))


def swa_mixer(h, wq, wk, wv, q_norm_w, k_norm_w, sinks, wo):
    b, l, _ = h.shape
    kv, grp, d = SWA_KV_HEADS, SWA_HEADS // SWA_KV_HEADS, SWA_HEAD_DIM
    q = rms_norm((h @ wq).reshape(b, l, kv, grp, d), q_norm_w) * (d ** -0.5)
    k = rms_norm((h @ wk).reshape(b, l, kv, d), k_norm_w)
    v = (h @ wv).reshape(b, l, kv, d)
    sink = sinks.astype(jnp.float32).reshape(kv, grp)
    qm, qr = q[:, :N_META], q[:, N_META:]
    km, kr = k[:, :N_META], k[:, N_META:]
    vm, vr = v[:, :N_META], v[:, N_META:]
    sm = jnp.einsum('bikgd,bjkd->bkgij', qm, km).astype(jnp.float32)
    mmask = jnp.tril(jnp.ones((N_META, N_META), bool))
    pm = sink_softmax(jnp.where(mmask, sm, -jnp.inf), sink[None, :, :, None, None])
    om = jnp.einsum('bkgij,bjkd->bikgd', pm.astype(v.dtype), vm).reshape(b, N_META, SWA_HEADS * d)
    s_real = l - N_META
    nb = s_real // SWA_BLOCK
    qb = qr.reshape(b, nb, SWA_BLOCK, kv, grp, d)
    kb = kr.reshape(b, nb, SWA_BLOCK, kv, d)
    vb = vr.reshape(b, nb, SWA_BLOCK, kv, d)
    band = lambda t: jnp.concatenate(
        [jnp.concatenate([jnp.zeros_like(t[:, :1]), t[:, :-1]], 1), t], 2)
    kband, vband = band(kb), band(vb)
    s_meta = jnp.einsum('bnikgd,bjkd->bnkgij', qb, km).astype(jnp.float32)
    s_band = jnp.einsum('bnikgd,bnjkd->bnkgij', qb, kband).astype(jnp.float32)
    i = jnp.arange(SWA_BLOCK)[:, None]
    j = jnp.arange(2 * SWA_BLOCK)[None, :]
    n = jnp.arange(nb)[:, None, None]
    rel = i + SWA_BLOCK - j
    valid = (rel >= 0) & (rel < SWA_WINDOW) & (n * SWA_BLOCK - SWA_BLOCK + j >= 0)
    s_band = jnp.where(valid[None, :, None, None], s_band, -jnp.inf)
    p = sink_softmax(jnp.concatenate([s_meta, s_band], -1), sink[None, None, :, :, None, None])
    p = p.astype(v.dtype)
    orr = (jnp.einsum('bnkgim,bmkd->bnikgd', p[..., :N_META], vm)
           + jnp.einsum('bnkgij,bnjkd->bnikgd', p[..., N_META:], vband))
    orr = orr.reshape(b, s_real, SWA_HEADS * d)
    return jnp.concatenate([om, orr], 1) @ wo


def conv_ffn(h, w_up, conv_w, w_down):
    gate, val = jnp.split(h @ w_up, 2, -1)
    gate = causal_dwconv(gate, conv_w)
    return (jax.nn.silu(gate) * val) @ w_down


def setup_inputs(seed: int = 0) -> dict:
    key = jax.random.key(seed)
    ks = jax.random.split(key, 24)
    nrm = lambda k, s, scale: jax.random.normal(k, s, jnp.float32) * scale
    gain = lambda k, s: 1.0 + 0.05 * jax.random.normal(k, s, jnp.float32)
    dt = jnp.exp(jax.random.uniform(ks[7], (N_EVEN, DN_HEADS), jnp.float32, np.log(1e-3), np.log(1e-1)))
    return {
        "x": nrm(ks[0], (BATCH, SEQ, D_MODEL), 1.0),
        "meta_tokens": nrm(ks[1], (N_META, D_MODEL), 1.0),
        "attn_norm_w": gain(ks[2], (DEPTH, D_MODEL)),
        "ffn_norm_w": gain(ks[3], (DEPTH, D_MODEL)),
        "mix_w_in": nrm(ks[4], (N_EVEN, D_MODEL, IN_DIM), D_MODEL ** -0.5),
        "conv_a_w": nrm(ks[5], (N_EVEN, CONV_A_WIDTH, D_CONV), CONV_A_WIDTH ** -0.5),
        "dn_conv_w": nrm(ks[6], (N_EVEN, DN_CONV_WIDTH, 3 * DN_DIM), DN_CONV_WIDTH ** -0.5),
        "dn_a_log": jnp.log(jax.random.uniform(ks[8], (N_EVEN, DN_HEADS), jnp.float32, 1.0, 16.0)),
        "dn_dt_bias": dt + jnp.log(-jnp.expm1(-dt)),
        "dn_norm_w": gain(ks[9], (N_EVEN, DN_HEAD_DIM)),
        "mix_w_out": nrm(ks[10], (N_EVEN, D_MIX, D_MODEL), D_MIX ** -0.5),
        "swa_wq": nrm(ks[11], (N_ODD, D_MODEL, SWA_HEADS * SWA_HEAD_DIM), D_MODEL ** -0.5),
        "swa_wk": nrm(ks[12], (N_ODD, D_MODEL, SWA_KV_HEADS * SWA_HEAD_DIM), D_MODEL ** -0.5),
        "swa_wv": nrm(ks[13], (N_ODD, D_MODEL, SWA_KV_HEADS * SWA_HEAD_DIM), D_MODEL ** -0.5),
        "swa_q_norm_w": gain(ks[14], (N_ODD, SWA_HEAD_DIM)),
        "swa_k_norm_w": gain(ks[15], (N_ODD, SWA_HEAD_DIM)),
        "swa_sinks": nrm(ks[16], (N_ODD, SWA_HEADS), 0.5),
        "swa_wo": nrm(ks[17], (N_ODD, SWA_HEADS * SWA_HEAD_DIM, D_MODEL), (SWA_HEADS * SWA_HEAD_DIM) ** -0.5),
        "ffn_w_up": nrm(ks[18], (DEPTH, D_MODEL, 2 * D_FF), D_MODEL ** -0.5),
        "ffn_conv_w": nrm(ks[19], (DEPTH, FFN_CONV_WIDTH, D_FF), FFN_CONV_WIDTH ** -0.5),
        "ffn_w_down": nrm(ks[20], (DEPTH, D_FF, D_MODEL), D_FF ** -0.5),
    }


def reference(x, meta_tokens, attn_norm_w, ffn_norm_w, mix_w_in, conv_a_w, dn_conv_w, dn_a_log,
              dn_dt_bias, dn_norm_w, mix_w_out, swa_wq, swa_wk, swa_wv, swa_q_norm_w, swa_k_norm_w,
              swa_sinks, swa_wo, ffn_w_up, ffn_conv_w, ffn_w_down):
    b = x.shape[0]
    meta = jnp.broadcast_to(meta_tokens[None].astype(x.dtype), (b, N_META, D_MODEL))
    h = jnp.concatenate([meta, x], 1)
    for layer in range(DEPTH):
        i = layer // 2
        hn = rms_norm(h, attn_norm_w[layer])
        if layer % 2 == 0:
            mix = even_mixer(hn, mix_w_in[i], conv_a_w[i], dn_conv_w[i], dn_a_log[i],
                             dn_dt_bias[i], dn_norm_w[i], mix_w_out[i])
        else:
            mix = swa_mixer(hn, swa_wq[i], swa_wk[i], swa_wv[i], swa_q_norm_w[i],
                            swa_k_norm_w[i], swa_sinks[i], swa_wo[i])
        h = h + mix.astype(h.dtype)
        ff = conv_ffn(rms_norm(h, ffn_norm_w[layer]), ffn_w_up[layer], ffn_conv_w[layer], ffn_w_down[layer])
        h = h + ff.astype(h.dtype)
    return h[:, N_META:]
```

```python
import functools

import jax
import jax.numpy as jnp
from jax import lax
from jax.experimental import pallas as pl
from jax.experimental.pallas import tpu as pltpu

F32 = jnp.float32
BF16 = jnp.bfloat16

D_MODEL = 1024
N_META = 16
D_CONV = 512
DN_HEADS = 4
DN_HEAD_DIM = 128
DN_DIM = DN_HEADS * DN_HEAD_DIM
DN_CHUNK = 64
SWA_HEADS = 16
SWA_KV_HEADS = 4
SWA_GROUP = SWA_HEADS // SWA_KV_HEADS
SWA_HEAD_DIM = 64
SWA_BLOCK = 128
D_FF = 2816
EPS = 1e-6

LANES = 128
ROW0 = 128
PAD_ROWS = ROW0 - N_META
P_MAIN = 3 * D_CONV + 4 * DN_DIM
NEG = -1e30
VMEM_LIMIT = 56 * 1024 * 1024


def _cparams(sem):
    return pltpu.CompilerParams(dimension_semantics=sem, vmem_limit_bytes=VMEM_LIMIT)


def _rms(x, w):
    ms = jnp.mean(x * x, axis=-1, keepdims=True)
    return x * lax.rsqrt(ms + EPS) * w


def _sigmoid(x):
    return 1.0 / (1.0 + jnp.exp(-x))


def _silu(x):
    return x * _sigmoid(x)


def _dot(a, b):
    return jnp.dot(a, b, preferred_element_type=F32)


def _dot_nt(a, b):
    return lax.dot_general(a, b, (((1,), (1,)), ((), ())), preferred_element_type=F32)


def _dot_tn(a, b):
    return lax.dot_general(a, b, (((0,), (0,)), ((), ())), preferred_element_type=F32)


def _split2(x):
    hi = x.astype(BF16)
    lo = (x - hi.astype(F32)).astype(BF16)
    return hi, lo


def _dot3(a, b):
    ah, al = _split2(a)
    bh, bl = _split2(b)
    return _dot(ah, bh) + (_dot(ah, bl) + _dot(al, bh))


def _norm_matmul_kernel(x_ref, nw_ref, w_ref, o_ref, xn_ref):
    @pl.when(pl.program_id(1) == 0)
    def _():
        xn_ref[...] = _rms(x_ref[...], nw_ref[...]).astype(BF16)

    o_ref[...] = _dot(xn_ref[...], w_ref[...]).astype(o_ref.dtype)


def _norm_matmul(x, nw, w, *, tm, tn, out_dtype=BF16):
    m, k = x.shape
    n = w.shape[1]
    return pl.pallas_call(
        _norm_matmul_kernel,
        grid=(m // tm, n // tn),
        in_specs=[pl.BlockSpec((tm, k), lambda i, j: (i, 0)),
                  pl.BlockSpec((1, k), lambda i, j: (0, 0)),
                  pl.BlockSpec((k, tn), lambda i, j: (0, j))],
        out_specs=pl.BlockSpec((tm, tn), lambda i, j: (i, j)),
        out_shape=jax.ShapeDtypeStruct((m, n), out_dtype),
        scratch_shapes=[pltpu.VMEM((tm, k), BF16)],
        compiler_params=_cparams(("parallel", "arbitrary")),
        name="norm_matmul",
    )(x, nw, w)


def _in_proj_kernel(x_ref, nw_ref, w_ref, wg_ref, o_ref, og_ref, xn_ref):
    @pl.when(pl.program_id(1) == 0)
    def _():
        xn = _rms(x_ref[...], nw_ref[...]).astype(BF16)
        xn_ref[...] = xn
        og_ref[...] = _dot(xn, wg_ref[...])

    o_ref[...] = _dot(xn_ref[...], w_ref[...]).astype(o_ref.dtype)


def _in_proj(x, nw, w, wg, *, tm, tn):
    m, k = x.shape
    n = w.shape[1]
    return pl.pallas_call(
        _in_proj_kernel,
        grid=(m // tm, n // tn),
        in_specs=[pl.BlockSpec((tm, k), lambda i, j: (i, 0)),
                  pl.BlockSpec((1, k), lambda i, j: (0, 0)),
                  pl.BlockSpec((k, tn), lambda i, j: (0, j)),
                  pl.BlockSpec((k, LANES), lambda i, j: (0, 0))],
        out_specs=[pl.BlockSpec((tm, tn), lambda i, j: (i, j)),
                   pl.BlockSpec((tm, LANES), lambda i, j: (i, 0))],
        out_shape=[jax.ShapeDtypeStruct((m, n), BF16),
                   jax.ShapeDtypeStruct((m, LANES), F32)],
        scratch_shapes=[pltpu.VMEM((tm, k), BF16)],
        compiler_params=_cparams(("parallel", "arbitrary")),
        name="in_proj",
    )(x, nw, w, wg)


def _matmul_res_kernel(x_ref, w_ref, r_ref, o_ref):
    o_ref[...] = r_ref[...] + _dot(x_ref[...], w_ref[...])


def _matmul_res(x, w, res, *, tm):
    m, k = x.shape
    n = w.shape[1]
    return pl.pallas_call(
        _matmul_res_kernel,
        grid=(m // tm,),
        in_specs=[pl.BlockSpec((tm, k), lambda i: (i, 0)),
                  pl.BlockSpec((k, n), lambda i: (0, 0)),
                  pl.BlockSpec((tm, n), lambda i: (i, 0))],
        out_specs=pl.BlockSpec((tm, n), lambda i: (i, 0)),
        out_shape=jax.ShapeDtypeStruct((m, n), F32),
        input_output_aliases={2: 0},
        compiler_params=_cparams(("parallel",)),
        name="matmul_res",
    )(x, w, res)


def _ffn_up_kernel(x_ref, nw_ref, wg_ref, wv_ref, cw_ref, o_ref, xn_ref, gs_ref, carry_ref):
    t = pl.program_id(1)
    j = pl.program_id(2)
    tm = x_ref.shape[1]

    @pl.when(j == 0)
    def _():
        xn_ref[...] = _rms(x_ref[0], nw_ref[...]).astype(BF16)

    xn = xn_ref[...]
    gate = _dot(xn, wg_ref[...])
    val = _dot(xn, wv_ref[...])

    @pl.when(t == 0)
    def _():
        gs_ref[0:8, :] = jnp.zeros((8, gs_ref.shape[1]), F32)

    @pl.when(t > 0)
    def _():
        gs_ref[0:8, :] = carry_ref[j]

    gs_ref[8:8 + tm, :] = gate
    conv = (cw_ref[0:1, :] * gs_ref[6:6 + tm, :]
            + cw_ref[1:2, :] * gs_ref[7:7 + tm, :]
            + cw_ref[2:3, :] * gate)
    carry_ref[j] = gs_ref[tm:tm + 8, :]
    o_ref[0] = (_silu(conv) * val).astype(o_ref.dtype)


def _ffn_up(h, nw, wg, wv, cw, *, tm, tf):
    b, lp, d = h.shape
    f = wg.shape[1]
    nf = f // tf
    return pl.pallas_call(
        _ffn_up_kernel,
        grid=(b, lp // tm, nf),
        in_specs=[pl.BlockSpec((1, tm, d), lambda bi, t, j: (bi, t, 0)),
                  pl.BlockSpec((1, d), lambda bi, t, j: (0, 0)),
                  pl.BlockSpec((d, tf), lambda bi, t, j: (0, j)),
                  pl.BlockSpec((d, tf), lambda bi, t, j: (0, j)),
                  pl.BlockSpec((3, tf), lambda bi, t, j: (0, j))],
        out_specs=pl.BlockSpec((1, tm, tf), lambda bi, t, j: (bi, t, j)),
        out_shape=jax.ShapeDtypeStruct((b, lp, f), BF16),
        scratch_shapes=[pltpu.VMEM((tm, d), BF16),
                        pltpu.VMEM((tm + 8, tf), F32),
                        pltpu.VMEM((nf, 8, tf), F32)],
        compiler_params=_cparams(("arbitrary", "arbitrary", "arbitrary")),
        name="ffn_up",
    )(h, nw, wg, wv, cw)


MIX_ROWS = 2 * DN_CHUNK


def _mixer_kernel(p_ref, gt_ref, sel_ref, conva_ref, dnconv_ref, alog_ref, dtb_ref, dnw_ref,
                  o_ref, xa_ref, xq_ref, s_ref):
    r = MIX_ROWS
    c = DN_CHUNK
    t = pl.program_id(1)

    @pl.when(t == 0)
    def _():
        xa_ref[0:8, :] = jnp.zeros((8, xa_ref.shape[1]), F32)
        xq_ref[0:8, :] = jnp.zeros((8, xq_ref.shape[1]), F32)
        s_ref[...] = jnp.zeros(s_ref.shape, F32)

    xa_ref[8:8 + r, :] = (p_ref[0, :, 0:D_CONV].astype(F32)
                          * p_ref[0, :, 2 * D_CONV:3 * D_CONV].astype(F32))
    ca = (conva_ref[0:1, :] * xa_ref[6:6 + r, :]
          + conva_ref[1:2, :] * xa_ref[7:7 + r, :]
          + conva_ref[2:3, :] * xa_ref[8:8 + r, :])
    o_ref[0, :, 0:D_CONV] = (p_ref[0, :, D_CONV:2 * D_CONV].astype(F32) * ca).astype(o_ref.dtype)
    xa_ref[0:8, :] = xa_ref[r:r + 8, :]

    q0 = 3 * D_CONV
    xq_ref[8:8 + r, :] = p_ref[0, :, q0:q0 + 3 * DN_DIM].astype(F32)
    cq = (dnconv_ref[0:1, :] * xq_ref[5:5 + r, :]
          + dnconv_ref[1:2, :] * xq_ref[6:6 + r, :]
          + dnconv_ref[2:3, :] * xq_ref[7:7 + r, :]
          + dnconv_ref[3:4, :] * xq_ref[8:8 + r, :])
    xq_ref[0:8, :] = xq_ref[r:r + 8, :]
    qkv = _silu(cq)

    gt = gt_ref[0]
    row = t * r + lax.broadcasted_iota(jnp.int32, (r, LANES), 0)
    live = row >= PAD_ROWS
    beta_all = jnp.where(live, _sigmoid(gt), 0.0)
    xg = gt + dtb_ref[...]
    softplus = jnp.maximum(xg, 0.0) + jnp.log1p(jnp.exp(-jnp.abs(xg)))
    g_all = jnp.where(live, -jnp.exp(alog_ref[...]) * softplus, 0.0)

    g_hi = g_all.astype(BF16)
    g_r1 = g_all - g_hi.astype(F32)
    g_mid = g_r1.astype(BF16)
    g_lo = (g_r1 - g_mid.astype(F32)).astype(BF16)
    sel = sel_ref[...]
    cums = _dot(sel, g_hi) + (_dot(sel, g_mid) + _dot(sel, g_lo))
    cdec = cums[0:r]
    dlast = (cums[r:2 * r], cums[2 * r:3 * r])
    cdec_t = cdec.T

    ii = lax.broadcasted_iota(jnp.int32, (r, r), 0)
    jj = lax.broadcasted_iota(jnp.int32, (r, r), 1)
    same = (ii >= c) == (jj >= c)
    incl = same & (ii >= jj)
    strict = same & (ii > jj)
    eye = (ii == jj).astype(F32)
    first = lax.broadcasted_iota(jnp.int32, (r, 1), 0) < c

    z0 = 3 * D_CONV + 3 * DN_DIM
    for h in range(DN_HEADS):
        hs = slice(h * DN_HEAD_DIM, (h + 1) * DN_HEAD_DIM)
        q = qkv[:, h * DN_HEAD_DIM:(h + 1) * DN_HEAD_DIM]
        k = qkv[:, DN_DIM + h * DN_HEAD_DIM:DN_DIM + (h + 1) * DN_HEAD_DIM]
        v = qkv[:, 2 * DN_DIM + h * DN_HEAD_DIM:2 * DN_DIM + (h + 1) * DN_HEAD_DIM]
        q = q * lax.rsqrt(jnp.sum(q * q, -1, keepdims=True) + EPS) * (DN_HEAD_DIM ** -0.5)
        k = k * lax.rsqrt(jnp.sum(k * k, -1, keepdims=True) + EPS)
        beta = beta_all[:, h:h + 1]
        cd_col = cdec[:, DN_HEADS + h:DN_HEADS + h + 1]
        cd_row = cdec_t[DN_HEADS + h:DN_HEADS + h + 1, :]
        dmask = jnp.where(incl, jnp.exp(jnp.where(incl, cd_col - cd_row, 0.0)), 0.0)
        kb = k.astype(BF16)
        kk = _dot_nt(kb, kb)
        a = jnp.where(strict, beta * kk * dmask, 0.0)
        tinv = eye - a
        pw = a
        for _ in range(5):
            pw = _dot3(pw, pw)
            tinv = tinv + _dot3(tinv, pw)
        e_cd = jnp.exp(cd_col)
        rhs = jnp.concatenate([v * beta, k * (beta * e_cd)], axis=1)
        sol = _dot3(tinv, rhs)
        u = sol[:, 0:DN_HEAD_DIM]
        w = sol[:, DN_HEAD_DIM:2 * DN_HEAD_DIM]
        qk = (_dot_nt(q.astype(BF16), kb) * dmask).astype(BF16)
        q_dec = q * e_cd
        dl_own = jnp.where(first, dlast[0][:, DN_HEADS + h:DN_HEADS + h + 1],
                           dlast[1][:, DN_HEADS + h:DN_HEADS + h + 1])
        k_dec = (k * jnp.exp(dl_own - cd_col)).astype(BF16)

        s = s_ref[h]
        outs = []
        for ci in range(2):
            rs = slice(ci * c, (ci + 1) * c)
            wq = jnp.concatenate([w[rs], q_dec[rs]], axis=0).astype(BF16)
            ws_qs = _dot(wq, s.astype(BF16))
            v_new = u[rs] - ws_qs[0:c]
            zeros = jnp.zeros((c, DN_HEAD_DIM), F32)
            v_full = jnp.concatenate([v_new, zeros] if ci == 0 else [zeros, v_new], axis=0)
            outs.append(ws_qs[c:2 * c] + _dot(qk[rs], v_full.astype(BF16)))
            g_last = jnp.exp(dlast[ci][:, DN_HEADS + h:DN_HEADS + h + 1])
            s = s * g_last + _dot_tn(k_dec[rs], v_new.astype(BF16))
        s_ref[h] = s
        o = jnp.concatenate(outs, axis=0)
        z = p_ref[0, :, z0 + h * DN_HEAD_DIM:z0 + (h + 1) * DN_HEAD_DIM].astype(F32)
        y = _rms(o, dnw_ref[...]) * _silu(z)
        o_ref[0, :, D_CONV + h * DN_HEAD_DIM:D_CONV + (h + 1) * DN_HEAD_DIM] = y.astype(o_ref.dtype)


def _decay_selectors():
    r, c = MIX_ROWS, DN_CHUNK
    i = jnp.arange(r)[:, None]
    j = jnp.arange(r)[None, :]
    tri = ((i // c) == (j // c)) & (i >= j)
    last0 = jnp.broadcast_to(j < c, (r, r))
    last1 = jnp.broadcast_to(j >= c, (r, r))
    return jnp.concatenate([tri, last0, last1], axis=0).astype(BF16)


def _mixer(p, gt, conv_a_w, dn_conv_w, a_log, dt_bias, dn_norm_w):
    b, lp, _ = p.shape
    r = MIX_ROWS
    pad_lanes = lambda vec: jnp.zeros((1, LANES), F32).at[0, DN_HEADS:2 * DN_HEADS].set(vec.astype(F32))
    const = lambda shape: pl.BlockSpec(shape, lambda bi, t: (0,) * len(shape))
    return pl.pallas_call(
        _mixer_kernel,
        grid=(b, lp // r),
        in_specs=[pl.BlockSpec((1, r, P_MAIN), lambda bi, t: (bi, t, 0)),
                  pl.BlockSpec((1, r, LANES), lambda bi, t: (bi, t, 0)),
                  const((3 * r, r)),
                  const((3, D_CONV)),
                  const((4, 3 * DN_DIM)),
                  const((1, LANES)),
                  const((1, LANES)),
                  const((1, DN_HEAD_DIM))],
        out_specs=pl.BlockSpec((1, r, D_CONV + DN_DIM), lambda bi, t: (bi, t, 0)),
        out_shape=jax.ShapeDtypeStruct((b, lp, D_CONV + DN_DIM), BF16),
        scratch_shapes=[pltpu.VMEM((r + 8, D_CONV), F32),
                        pltpu.VMEM((r + 8, 3 * DN_DIM), F32),
                        pltpu.VMEM((DN_HEADS, DN_HEAD_DIM, DN_HEAD_DIM), F32)],
        compiler_params=_cparams(("arbitrary", "arbitrary")),
        name="gdn_mixer",
    )(p, gt, _decay_selectors(), conv_a_w.astype(F32), dn_conv_w.astype(F32),
      pad_lanes(a_log), pad_lanes(dt_bias), dn_norm_w.astype(F32).reshape(1, DN_HEAD_DIM))


def _head_norm(x, w):
    parts = []
    for hd in range(x.shape[1] // SWA_HEAD_DIM):
        xh = x[:, hd * SWA_HEAD_DIM:(hd + 1) * SWA_HEAD_DIM].astype(F32)
        parts.append(_rms(xh, w))
    return parts


def _swa_kernel(sink_ref, q_ref, km_ref, kp_ref, kc_ref, vm_ref, vp_ref, vc_ref, qnw_ref, knw_ref,
                o_ref):
    i = pl.program_id(1)
    blk = SWA_BLOCK
    d = SWA_HEAD_DIM
    qh = [x * (d ** -0.5) for x in _head_norm(q_ref[0], qnw_ref[...])]
    kparts = [_head_norm(r[0], knw_ref[...]) for r in (km_ref, kp_ref, kc_ref)]

    nq = SWA_GROUP * blk
    rr = lax.broadcasted_iota(jnp.int32, (nq, 3 * blk), 0) % blk
    cc = lax.broadcasted_iota(jnp.int32, (nq, 3 * blk), 1)
    meta_ok = (cc >= PAD_ROWS) & (cc < blk) & ((i > 0) | (cc <= rr))
    prev_ok = (cc >= blk) & (cc < 2 * blk) & (cc - blk > rr) & (i >= 2)
    cur_ok = (cc >= 2 * blk) & (cc - 2 * blk <= rr) & (i >= 1)
    valid = meta_ok | prev_ok | cur_ok
    grp = lax.broadcasted_iota(jnp.int32, (nq, 1), 0) // blk

    outs = []
    for kh in range(SWA_KV_HEADS):
        qs = jnp.concatenate([qh[kh * SWA_GROUP + g] for g in range(SWA_GROUP)], axis=0).astype(BF16)
        ks = jnp.concatenate([kp[kh] for kp in kparts], axis=0).astype(BF16)
        vs = jnp.concatenate([r[0, :, kh * d:(kh + 1) * d] for r in (vm_ref, vp_ref, vc_ref)], axis=0)
        s = jnp.where(valid, _dot_nt(qs, ks), NEG)
        sink = jnp.zeros((nq, 1), F32)
        for g in range(SWA_GROUP):
            sink = jnp.where(grp == g, sink_ref[kh * SWA_GROUP + g], sink)
        m = jnp.maximum(jnp.max(s, axis=-1, keepdims=True), sink)
        e = jnp.exp(s - m)
        den = jnp.sum(e, axis=-1, keepdims=True) + jnp.exp(sink - m)
        p = (e / den).astype(BF16)
        o = _dot(p, vs)
        outs.extend(o[g * blk:(g + 1) * blk] for g in range(SWA_GROUP))
    o_ref[0] = jnp.concatenate(outs, axis=1).astype(o_ref.dtype)


def _swa(qkv, sinks, q_norm_w, k_norm_w):
    b, lp, _ = qkv.shape
    blk = SWA_BLOCK
    nq = SWA_HEADS * SWA_HEAD_DIM
    nkv = SWA_KV_HEADS * SWA_HEAD_DIM
    kcol = nq // nkv
    vcol = kcol + 1
    prev = lambda t: jnp.maximum(t - 1, 0)
    return pl.pallas_call(
        _swa_kernel,
        grid=(b, lp // blk),
        in_specs=[pl.BlockSpec(memory_space=pltpu.SMEM),
                  pl.BlockSpec((1, blk, nq), lambda bi, t: (bi, t, 0)),
                  pl.BlockSpec((1, blk, nkv), lambda bi, t: (bi, 0, kcol)),
                  pl.BlockSpec((1, blk, nkv), lambda bi, t: (bi, prev(t), kcol)),
                  pl.BlockSpec((1, blk, nkv), lambda bi, t: (bi, t, kcol)),
                  pl.BlockSpec((1, blk, nkv), lambda bi, t: (bi, 0, vcol)),
                  pl.BlockSpec((1, blk, nkv), lambda bi, t: (bi, prev(t), vcol)),
                  pl.BlockSpec((1, blk, nkv), lambda bi, t: (bi, t, vcol)),
                  pl.BlockSpec((1, SWA_HEAD_DIM), lambda bi, t: (0, 0)),
                  pl.BlockSpec((1, SWA_HEAD_DIM), lambda bi, t: (0, 0))],
        out_specs=pl.BlockSpec((1, blk, nq), lambda bi, t: (bi, t, 0)),
        out_shape=jax.ShapeDtypeStruct((b, lp, nq), BF16),
        compiler_params=_cparams(("parallel", "arbitrary")),
        name="swa",
    )(sinks.astype(F32), qkv, qkv, qkv, qkv, qkv, qkv, qkv,
      q_norm_w.astype(F32).reshape(1, SWA_HEAD_DIM), k_norm_w.astype(F32).reshape(1, SWA_HEAD_DIM))


def _ffn(h, nw, w_up, conv_w, w_down):
    b, lp, d = h.shape
    wg = w_up[:, :D_FF].astype(BF16)
    wv = w_up[:, D_FF:].astype(BF16)
    act = _ffn_up(h, nw.reshape(1, d), wg, wv, conv_w.astype(F32), tm=704, tf=1408)
    out = _matmul_res(act.reshape(b * lp, D_FF), w_down.astype(BF16), h.reshape(b * lp, d), tm=512)
    return out.reshape(b, lp, d)


def kernel(x, meta_tokens, attn_norm_w, ffn_norm_w, mix_w_in, conv_a_w, dn_conv_w, dn_a_log,
           dn_dt_bias, dn_norm_w, mix_w_out, swa_wq, swa_wk, swa_wv, swa_q_norm_w, swa_k_norm_w,
           swa_sinks, swa_wo, ffn_w_up, ffn_conv_w, ffn_w_down):
    b, seq, d = x.shape
    lp = ROW0 + seq
    meta = jnp.broadcast_to(meta_tokens[None].astype(x.dtype), (b, N_META, d))
    h = jnp.concatenate([jnp.zeros((b, PAD_ROWS, d), x.dtype), meta, x], axis=1)
    depth = attn_norm_w.shape[0]
    for layer in range(depth):
        i = layer // 2
        nw = attn_norm_w[layer].reshape(1, d)
        hf = h.reshape(b * lp, d)
        if layer % 2 == 0:
            w_in = mix_w_in[i]
            wg = jnp.zeros((d, LANES), BF16).at[:, :2 * DN_HEADS].set(w_in[:, P_MAIN:].astype(BF16))
            p, gt = _in_proj(hf, nw, w_in[:, :P_MAIN].astype(BF16), wg, tm=1024, tn=512)
            y = _mixer(p.reshape(b, lp, P_MAIN), gt.reshape(b, lp, LANES), conv_a_w[i], dn_conv_w[i],
                       dn_a_log[i], dn_dt_bias[i], dn_norm_w[i])
            hf = _matmul_res(y.reshape(b * lp, -1), mix_w_out[i].astype(BF16), hf, tm=1024)
        else:
            wqkv = jnp.concatenate([swa_wq[i], swa_wk[i], swa_wv[i]], axis=1).astype(BF16)
            qkv = _norm_matmul(hf, nw, wqkv, tm=1024, tn=512)
            att = _swa(qkv.reshape(b, lp, -1), swa_sinks[i], swa_q_norm_w[i], swa_k_norm_w[i])
            hf = _matmul_res(att.reshape(b * lp, -1), swa_wo[i].astype(BF16), hf, tm=1024)
        h = _ffn(hf.reshape(b, lp, d), ffn_norm_w[layer], ffn_w_up[layer], ffn_conv_w[layer],
                 ffn_w_down[layer])
    return h[:, ROW0:]
```

```python
import functools

import jax
import jax.numpy as jnp
from jax import lax
from jax.experimental import pallas as pl
from jax.experimental.pallas import tpu as pltpu

F32 = jnp.float32
BF16 = jnp.bfloat16

D_MODEL = 1024
N_META = 16
D_CONV = 512
DN_HEADS = 4
DN_HEAD_DIM = 128
DN_DIM = DN_HEADS * DN_HEAD_DIM
DN_CHUNK = 64
SWA_HEADS = 16
SWA_KV_HEADS = 4
SWA_GROUP = SWA_HEADS // SWA_KV_HEADS
SWA_HEAD_DIM = 64
SWA_BLOCK = 128
D_FF = 2816
EPS = 1e-6

LANES = 128
ROW0 = 128
PAD_ROWS = ROW0 - N_META
P_MAIN = 3 * D_CONV + 4 * DN_DIM
NEG = -1e30
VMEM_LIMIT = 56 * 1024 * 1024


def _cparams(sem):
    return pltpu.CompilerParams(dimension_semantics=sem, vmem_limit_bytes=VMEM_LIMIT)


def _rms(x, w):
    ms = jnp.mean(x * x, axis=-1, keepdims=True)
    return x * lax.rsqrt(ms + EPS) * w


def _sigmoid(x):
    return 1.0 / (1.0 + jnp.exp(-x))


def _silu(x):
    return x * _sigmoid(x)


def _dot(a, b):
    return jnp.dot(a, b, preferred_element_type=F32)


def _dot_nt(a, b):
    return lax.dot_general(a, b, (((1,), (1,)), ((), ())), preferred_element_type=F32)


def _dot_tn(a, b):
    return lax.dot_general(a, b, (((0,), (0,)), ((), ())), preferred_element_type=F32)


def _norm_matmul_kernel(x_ref, nw_ref, w_ref, o_ref, xn_ref):
    @pl.when(pl.program_id(1) == 0)
    def _():
        xn_ref[...] = _rms(x_ref[...], nw_ref[...]).astype(BF16)

    o_ref[...] = _dot(xn_ref[...], w_ref[...]).astype(o_ref.dtype)


def _norm_matmul(x, nw, w, *, tm, tn, out_dtype=BF16):
    m, k = x.shape
    n = w.shape[1]
    return pl.pallas_call(
        _norm_matmul_kernel,
        grid=(m // tm, n // tn),
        in_specs=[pl.BlockSpec((tm, k), lambda i, j: (i, 0)),
                  pl.BlockSpec((1, k), lambda i, j: (0, 0)),
                  pl.BlockSpec((k, tn), lambda i, j: (0, j))],
        out_specs=pl.BlockSpec((tm, tn), lambda i, j: (i, j)),
        out_shape=jax.ShapeDtypeStruct((m, n), out_dtype),
        scratch_shapes=[pltpu.VMEM((tm, k), BF16)],
        compiler_params=_cparams(("parallel", "arbitrary")),
        name="norm_matmul",
    )(x, nw, w)


def _in_proj_kernel(x_ref, nw_ref, w_ref, wg_ref, o_ref, og_ref, xn_ref):
    @pl.when(pl.program_id(1) == 0)
    def _():
        xn = _rms(x_ref[...], nw_ref[...]).astype(BF16)
        xn_ref[...] = xn
        og_ref[...] = _dot(xn, wg_ref[...])

    o_ref[...] = _dot(xn_ref[...], w_ref[...]).astype(o_ref.dtype)


def _in_proj(x, nw, w, wg, *, tm, tn):
    m, k = x.shape
    n = w.shape[1]
    return pl.pallas_call(
        _in_proj_kernel,
        grid=(m // tm, n // tn),
        in_specs=[pl.BlockSpec((tm, k), lambda i, j: (i, 0)),
                  pl.BlockSpec((1, k), lambda i, j: (0, 0)),
                  pl.BlockSpec((k, tn), lambda i, j: (0, j)),
                  pl.BlockSpec((k, LANES), lambda i, j: (0, 0))],
        out_specs=[pl.BlockSpec((tm, tn), lambda i, j: (i, j)),
                   pl.BlockSpec((tm, LANES), lambda i, j: (i, 0))],
        out_shape=[jax.ShapeDtypeStruct((m, n), BF16),
                   jax.ShapeDtypeStruct((m, LANES), F32)],
        scratch_shapes=[pltpu.VMEM((tm, k), BF16)],
        compiler_params=_cparams(("parallel", "arbitrary")),
        name="in_proj",
    )(x, nw, w, wg)


def _matmul_res_kernel(x_ref, w_ref, r_ref, o_ref):
    o_ref[...] = r_ref[...] + _dot(x_ref[...], w_ref[...])


def _matmul_res(x, w, res, *, tm):
    m, k = x.shape
    n = w.shape[1]
    return pl.pallas_call(
        _matmul_res_kernel,
        grid=(m // tm,),
        in_specs=[pl.BlockSpec((tm, k), lambda i: (i, 0)),
                  pl.BlockSpec((k, n), lambda i: (0, 0)),
                  pl.BlockSpec((tm, n), lambda i: (i, 0))],
        out_specs=pl.BlockSpec((tm, n), lambda i: (i, 0)),
        out_shape=jax.ShapeDtypeStruct((m, n), F32),
        input_output_aliases={2: 0},
        compiler_params=_cparams(("parallel",)),
        name="matmul_res",
    )(x, w, res)


def _ffn_up_kernel(x_ref, nw_ref, wg_ref, wv_ref, cw_ref, o_ref, xn_ref, gs_ref, carry_ref):
    t = pl.program_id(1)
    j = pl.program_id(2)
    tm = x_ref.shape[1]

    @pl.when(j == 0)
    def _():
        xn_ref[...] = _rms(x_ref[0], nw_ref[...]).astype(BF16)

    xn = xn_ref[...]
    gate = _dot(xn, wg_ref[...])
    val = _dot(xn, wv_ref[...])

    @pl.when(t == 0)
    def _():
        gs_ref[0:8, :] = jnp.zeros((8, gs_ref.shape[1]), F32)

    @pl.when(t > 0)
    def _():
        gs_ref[0:8, :] = carry_ref[j]

    gs_ref[8:8 + tm, :] = gate
    conv = (cw_ref[0:1, :] * gs_ref[6:6 + tm, :]
            + cw_ref[1:2, :] * gs_ref[7:7 + tm, :]
            + cw_ref[2:3, :] * gate)
    carry_ref[j] = gs_ref[tm:tm + 8, :]
    o_ref[0] = (_silu(conv) * val).astype(o_ref.dtype)


def _ffn_up(h, nw, wg, wv, cw, *, tm, tf):
    b, lp, d = h.shape
    f = wg.shape[1]
    nf = f // tf
    return pl.pallas_call(
        _ffn_up_kernel,
        grid=(b, lp // tm, nf),
        in_specs=[pl.BlockSpec((1, tm, d), lambda bi, t, j: (bi, t, 0)),
                  pl.BlockSpec((1, d), lambda bi, t, j: (0, 0)),
                  pl.BlockSpec((d, tf), lambda bi, t, j: (0, j)),
                  pl.BlockSpec((d, tf), lambda bi, t, j: (0, j)),
                  pl.BlockSpec((3, tf), lambda bi, t, j: (0, j))],
        out_specs=pl.BlockSpec((1, tm, tf), lambda bi, t, j: (bi, t, j)),
        out_shape=jax.ShapeDtypeStruct((b, lp, f), BF16),
        scratch_shapes=[pltpu.VMEM((tm, d), BF16),
                        pltpu.VMEM((tm + 8, tf), F32),
                        pltpu.VMEM((nf, 8, tf), F32)],
        compiler_params=_cparams(("arbitrary", "arbitrary", "arbitrary")),
        name="ffn_up",
    )(h, nw, wg, wv, cw)


MIX_PAIR = 2 * DN_CHUNK
MIX_PAIRS = 3
CONV_CARRY = 16
MIX_ROWS = MIX_PAIR * MIX_PAIRS


def _mixer_kernel(p_ref, pa_ref, pq_ref, gt_ref, sel_ref, shift_ref, conva_ref, dnconv_ref, alog_ref,
                  dtb_ref, dnw_ref, o_ref, xa_ref, s_ref):
    r = MIX_ROWS
    pr = MIX_PAIR
    c = DN_CHUNK
    dh = DN_HEAD_DIM
    t = pl.program_id(1)
    has_prev = t > 0

    @pl.when(t == 0)
    def _():
        s_ref[...] = jnp.zeros(s_ref.shape, F32)

    prev_a = (pa_ref[0, CONV_CARRY - 8:CONV_CARRY, 0:D_CONV].astype(F32)
              * pa_ref[0, CONV_CARRY - 8:CONV_CARRY, 2 * D_CONV:3 * D_CONV].astype(F32))
    xa_ref[0:8, :] = jnp.where(has_prev, prev_a, 0.0)
    xa_ref[8:8 + r, :] = (p_ref[0, :, 0:D_CONV].astype(F32)
                          * p_ref[0, :, 2 * D_CONV:3 * D_CONV].astype(F32))
    ca = (conva_ref[0:1, :] * xa_ref[6:6 + r, :]
          + conva_ref[1:2, :] * xa_ref[7:7 + r, :]
          + conva_ref[2:3, :] * xa_ref[8:8 + r, :])
    o_ref[0, :, 0:D_CONV] = (p_ref[0, :, D_CONV:2 * D_CONV].astype(F32) * ca).astype(o_ref.dtype)

    q0 = 3 * D_CONV
    xq = p_ref[0, :, q0:q0 + 3 * DN_DIM]
    prev_q = jnp.where(has_prev, pq_ref[0], jnp.zeros((CONV_CARRY, 3 * DN_DIM), BF16))
    taps = _dot(shift_ref[...], jnp.concatenate([prev_q, xq], axis=0))
    cq = dnconv_ref[3:4, :] * xq.astype(F32)
    for j in range(3):
        cq = cq + dnconv_ref[j:j + 1, :] * taps[j * r:(j + 1) * r]
    qkv = _silu(cq)

    gt = gt_ref[0]
    row = t * r + lax.broadcasted_iota(jnp.int32, (r, LANES), 0)
    live = row >= PAD_ROWS
    beta_all = jnp.where(live, _sigmoid(gt), 0.0)
    xg = gt + dtb_ref[...]
    softplus = jnp.maximum(xg, 0.0) + jnp.log1p(jnp.exp(-jnp.abs(xg)))
    g_all = jnp.where(live, -jnp.exp(alog_ref[...]) * softplus, 0.0)
    g_hi = g_all.astype(BF16)
    g_r1 = g_all - g_hi.astype(F32)
    g_mid = g_r1.astype(BF16)
    g_lo = (g_r1 - g_mid.astype(F32)).astype(BF16)
    sel = sel_ref[...]

    ii = lax.broadcasted_iota(jnp.int32, (pr, pr), 0)
    jj = lax.broadcasted_iota(jnp.int32, (pr, pr), 1)
    same = (ii >= c) == (jj >= c)
    incl = same & (ii >= jj)
    strict = same & (ii > jj)
    first = lax.broadcasted_iota(jnp.int32, (pr, 1), 0) < c

    keys = [(pi, h) for pi in range(MIX_PAIRS) for h in range(DN_HEADS)]
    local = {}
    amat = {}
    for pi in range(MIX_PAIRS):
        ps = slice(pi * pr, (pi + 1) * pr)
        cums = _dot(sel, g_hi[ps]) + (_dot(sel, g_mid[ps]) + _dot(sel, g_lo[ps]))
        cdec = cums[0:pr]
        dlast = tuple(jnp.broadcast_to(cums[pr + 8 * ci:pr + 8 * ci + 1], (pr, LANES)) for ci in range(2))
        cdec_t = cdec.T
        for h in range(DN_HEADS):
            q = qkv[ps, h * dh:(h + 1) * dh]
            k = qkv[ps, DN_DIM + h * dh:DN_DIM + (h + 1) * dh]
            v = qkv[ps, 2 * DN_DIM + h * dh:2 * DN_DIM + (h + 1) * dh]
            q = q * lax.rsqrt(jnp.sum(q * q, -1, keepdims=True) + EPS) * (dh ** -0.5)
            k = k * lax.rsqrt(jnp.sum(k * k, -1, keepdims=True) + EPS)
            gl = DN_HEADS + h
            beta = beta_all[ps, h:h + 1]
            cd_col = cdec[:, gl:gl + 1]
            cd_row = cdec_t[gl:gl + 1, :]
            dmask = jnp.where(incl, jnp.exp(jnp.where(incl, cd_col - cd_row, 0.0)), 0.0)
            kb = k.astype(BF16)
            qk_kk = _dot_nt(jnp.concatenate([q.astype(BF16), kb], axis=0), kb)
            amat[pi, h] = jnp.where(strict, beta * qk_kk[pr:2 * pr] * dmask, 0.0)
            e_cd = jnp.exp(cd_col)
            dl_own = jnp.where(first, dlast[0][:, gl:gl + 1], dlast[1][:, gl:gl + 1])
            local[pi, h] = dict(
                rhs=jnp.concatenate([v * beta, k * (beta * e_cd)], axis=1),
                qk=(qk_kk[0:pr] * dmask).astype(BF16),
                q_dec=(q * e_cd).astype(BF16),
                k_dec_t=(k * jnp.exp(dl_own - cd_col)).T.astype(BF16),
                g_last=[jnp.exp(d[:, gl:gl + 1]) for d in dlast])

    tm = {key: -amat[key] for key in keys}
    pw = dict(amat)
    for _ in range(5):
        for key in keys:
            pwb = pw[key].astype(BF16)
            pw[key] = _dot(pwb, pwb)
        for key in keys:
            tm[key] = tm[key] + pw[key] + _dot(tm[key].astype(BF16), pw[key].astype(BF16))
    for key in keys:
        rhs = local[key]["rhs"]
        sol = rhs + _dot(tm[key].astype(BF16), rhs.astype(BF16))
        local[key]["u"] = sol[:, 0:dh]
        local[key]["w"] = sol[:, dh:2 * dh].astype(BF16)

    z0 = 3 * D_CONV + 3 * DN_DIM
    state = [s_ref[h] for h in range(DN_HEADS)]
    outs = [[] for _ in range(DN_HEADS)]
    zeros = jnp.zeros((c, dh), BF16)
    for pi in range(MIX_PAIRS):
        for ci in range(2):
            rs = slice(ci * c, (ci + 1) * c)
            ws_qs = [_dot(jnp.concatenate([local[pi, h]["w"][rs], local[pi, h]["q_dec"][rs]], axis=0),
                          state[h].astype(BF16)) for h in range(DN_HEADS)]
            v_full = []
            for h in range(DN_HEADS):
                vb = (local[pi, h]["u"][rs] - ws_qs[h][0:c]).astype(BF16)
                v_full.append(jnp.concatenate([vb, zeros] if ci == 0 else [zeros, vb], axis=0))
            for h in range(DN_HEADS):
                state[h] = (state[h] * local[pi, h]["g_last"][ci]
                            + _dot(local[pi, h]["k_dec_t"], v_full[h]))
            for h in range(DN_HEADS):
                outs[h].append(ws_qs[h][c:2 * c] + _dot(local[pi, h]["qk"][rs], v_full[h]))
    for h in range(DN_HEADS):
        s_ref[h] = state[h]
        o = jnp.concatenate(outs[h], axis=0)
        z = p_ref[0, :, z0 + h * dh:z0 + (h + 1) * dh].astype(F32)
        y = _rms(o, dnw_ref[...]) * _silu(z)
        o_ref[0, :, D_CONV + h * dh:D_CONV + (h + 1) * dh] = y.astype(o_ref.dtype)


def _decay_selectors():
    r, c = MIX_PAIR, DN_CHUNK
    i = jnp.arange(r)[:, None]
    j = jnp.arange(r)[None, :]
    tri = ((i // c) == (j // c)) & (i >= j)
    last0 = jnp.broadcast_to(j < c, (8, r))
    last1 = jnp.broadcast_to(j >= c, (8, r))
    return jnp.concatenate([tri, last0, last1], axis=0).astype(BF16)


def _shift_matrix(rows):
    t = jnp.arange(rows)[:, None]
    s = jnp.arange(CONV_CARRY + rows)[None, :]
    return jnp.concatenate([s == t + CONV_CARRY - d for d in (3, 2, 1)], axis=0).astype(BF16)


def _mixer(p, gt, conv_a_w, dn_conv_w, a_log, dt_bias, dn_norm_w):
    b, lp, _ = p.shape
    r = MIX_ROWS
    pad_lanes = lambda vec: jnp.zeros((1, LANES), F32).at[0, DN_HEADS:2 * DN_HEADS].set(vec.astype(F32))
    const = lambda shape: pl.BlockSpec(shape, lambda bi, t: (0,) * len(shape))
    prev_rows = lambda t: jnp.maximum(t * (r // CONV_CARRY) - 1, 0)
    qkv_cols = 3 * DN_DIM
    return pl.pallas_call(
        _mixer_kernel,
        grid=(b, lp // r),
        in_specs=[pl.BlockSpec((1, r, P_MAIN), lambda bi, t: (bi, t, 0)),
                  pl.BlockSpec((1, CONV_CARRY, qkv_cols), lambda bi, t: (bi, prev_rows(t), 0)),
                  pl.BlockSpec((1, CONV_CARRY, qkv_cols), lambda bi, t: (bi, prev_rows(t), 1)),
                  pl.BlockSpec((1, r, LANES), lambda bi, t: (bi, t, 0)),
                  const((MIX_PAIR + 16, MIX_PAIR)),
                  const((3 * r, CONV_CARRY + r)),
                  const((3, D_CONV)),
                  const((4, 3 * DN_DIM)),
                  const((1, LANES)),
                  const((1, LANES)),
                  const((1, DN_HEAD_DIM))],
        out_specs=pl.BlockSpec((1, r, D_CONV + DN_DIM), lambda bi, t: (bi, t, 0)),
        out_shape=jax.ShapeDtypeStruct((b, lp, D_CONV + DN_DIM), BF16),
        scratch_shapes=[pltpu.VMEM((r + 8, D_CONV), F32),
                        pltpu.VMEM((DN_HEADS, DN_HEAD_DIM, DN_HEAD_DIM), F32)],
        compiler_params=_cparams(("arbitrary", "arbitrary")),
        name="gdn_mixer",
    )(p, p, p, gt, _decay_selectors(), _shift_matrix(r), conv_a_w.astype(F32), dn_conv_w.astype(F32),
      pad_lanes(a_log), pad_lanes(dt_bias), dn_norm_w.astype(F32).reshape(1, DN_HEAD_DIM))


def _head_norm(x, w):
    parts = []
    for hd in range(x.shape[1] // SWA_HEAD_DIM):
        xh = x[:, hd * SWA_HEAD_DIM:(hd + 1) * SWA_HEAD_DIM].astype(F32)
        parts.append(_rms(xh, w))
    return parts


def _swa_kernel(sink_ref, q_ref, km_ref, kp_ref, kc_ref, vm_ref, vp_ref, vc_ref, qnw_ref, knw_ref,
                o_ref):
    i = pl.program_id(1)
    blk = SWA_BLOCK
    d = SWA_HEAD_DIM
    qh = [x * (d ** -0.5) for x in _head_norm(q_ref[0], qnw_ref[...])]
    kparts = [_head_norm(r[0], knw_ref[...]) for r in (km_ref, kp_ref, kc_ref)]

    nq = SWA_GROUP * blk
    rr = lax.broadcasted_iota(jnp.int32, (nq, 3 * blk), 0) % blk
    cc = lax.broadcasted_iota(jnp.int32, (nq, 3 * blk), 1)
    meta_ok = (cc >= PAD_ROWS) & (cc < blk) & ((i > 0) | (cc <= rr))
    prev_ok = (cc >= blk) & (cc < 2 * blk) & (cc - blk > rr) & (i >= 2)
    cur_ok = (cc >= 2 * blk) & (cc - 2 * blk <= rr) & (i >= 1)
    valid = meta_ok | prev_ok | cur_ok
    grp = lax.broadcasted_iota(jnp.int32, (nq, 1), 0) // blk

    outs = []
    for kh in range(SWA_KV_HEADS):
        qs = jnp.concatenate([qh[kh * SWA_GROUP + g] for g in range(SWA_GROUP)], axis=0).astype(BF16)
        ks = jnp.concatenate([kp[kh] for kp in kparts], axis=0).astype(BF16)
        vs = jnp.concatenate([r[0, :, kh * d:(kh + 1) * d] for r in (vm_ref, vp_ref, vc_ref)], axis=0)
        s = jnp.where(valid, _dot_nt(qs, ks), NEG)
        sink = jnp.zeros((nq, 1), F32)
        for g in range(SWA_GROUP):
            sink = jnp.where(grp == g, sink_ref[kh * SWA_GROUP + g], sink)
        m = jnp.maximum(jnp.max(s, axis=-1, keepdims=True), sink)
        e = jnp.exp(s - m)
        den = jnp.sum(e, axis=-1, keepdims=True) + jnp.exp(sink - m)
        p = (e / den).astype(BF16)
        o = _dot(p, vs)
        outs.extend(o[g * blk:(g + 1) * blk] for g in range(SWA_GROUP))
    o_ref[0] = jnp.concatenate(outs, axis=1).astype(o_ref.dtype)


def _swa(qkv, sinks, q_norm_w, k_norm_w):
    b, lp, _ = qkv.shape
    blk = SWA_BLOCK
    nq = SWA_HEADS * SWA_HEAD_DIM
    nkv = SWA_KV_HEADS * SWA_HEAD_DIM
    kcol = nq // nkv
    vcol = kcol + 1
    prev = lambda t: jnp.maximum(t - 1, 0)
    return pl.pallas_call(
        _swa_kernel,
        grid=(b, lp // blk),
        in_specs=[pl.BlockSpec(memory_space=pltpu.SMEM),
                  pl.BlockSpec((1, blk, nq), lambda bi, t: (bi, t, 0)),
                  pl.BlockSpec((1, blk, nkv), lambda bi, t: (bi, 0, kcol)),
                  pl.BlockSpec((1, blk, nkv), lambda bi, t: (bi, prev(t), kcol)),
                  pl.BlockSpec((1, blk, nkv), lambda bi, t: (bi, t, kcol)),
                  pl.BlockSpec((1, blk, nkv), lambda bi, t: (bi, 0, vcol)),
                  pl.BlockSpec((1, blk, nkv), lambda bi, t: (bi, prev(t), vcol)),
                  pl.BlockSpec((1, blk, nkv), lambda bi, t: (bi, t, vcol)),
                  pl.BlockSpec((1, SWA_HEAD_DIM), lambda bi, t: (0, 0)),
                  pl.BlockSpec((1, SWA_HEAD_DIM), lambda bi, t: (0, 0))],
        out_specs=pl.BlockSpec((1, blk, nq), lambda bi, t: (bi, t, 0)),
        out_shape=jax.ShapeDtypeStruct((b, lp, nq), BF16),
        compiler_params=_cparams(("parallel", "arbitrary")),
        name="swa",
    )(sinks.astype(F32), qkv, qkv, qkv, qkv, qkv, qkv, qkv,
      q_norm_w.astype(F32).reshape(1, SWA_HEAD_DIM), k_norm_w.astype(F32).reshape(1, SWA_HEAD_DIM))


def _ffn(h, nw, w_up, conv_w, w_down):
    b, lp, d = h.shape
    wg = w_up[:, :D_FF].astype(BF16)
    wv = w_up[:, D_FF:].astype(BF16)
    act = _ffn_up(h, nw.reshape(1, d), wg, wv, conv_w.astype(F32), tm=704, tf=1408)
    out = _matmul_res(act.reshape(b * lp, D_FF), w_down.astype(BF16), h.reshape(b * lp, d), tm=512)
    return out.reshape(b, lp, d)


def kernel(x, meta_tokens, attn_norm_w, ffn_norm_w, mix_w_in, conv_a_w, dn_conv_w, dn_a_log,
           dn_dt_bias, dn_norm_w, mix_w_out, swa_wq, swa_wk, swa_wv, swa_q_norm_w, swa_k_norm_w,
           swa_sinks, swa_wo, ffn_w_up, ffn_conv_w, ffn_w_down):
    b, seq, d = x.shape
    lp = ROW0 + seq
    meta = jnp.broadcast_to(meta_tokens[None].astype(x.dtype), (b, N_META, d))
    h = jnp.concatenate([jnp.zeros((b, PAD_ROWS, d), x.dtype), meta, x], axis=1)
    depth = attn_norm_w.shape[0]
    for layer in range(depth):
        i = layer // 2
        nw = attn_norm_w[layer].reshape(1, d)
        hf = h.reshape(b * lp, d)
        if layer % 2 == 0:
            w_in = mix_w_in[i]
            wg = jnp.zeros((d, LANES), BF16).at[:, :2 * DN_HEADS].set(w_in[:, P_MAIN:].astype(BF16))
            p, gt = _in_proj(hf, nw, w_in[:, :P_MAIN].astype(BF16), wg, tm=1024, tn=512)
            y = _mixer(p.reshape(b, lp, P_MAIN), gt.reshape(b, lp, LANES), conv_a_w[i], dn_conv_w[i],
                       dn_a_log[i], dn_dt_bias[i], dn_norm_w[i])
            hf = _matmul_res(y.reshape(b * lp, -1), mix_w_out[i].astype(BF16), hf, tm=1024)
        else:
            wqkv = jnp.concatenate([swa_wq[i], swa_wk[i], swa_wv[i]], axis=1).astype(BF16)
            qkv = _norm_matmul(hf, nw, wqkv, tm=1024, tn=512)
            att = _swa(qkv.reshape(b, lp, -1), swa_sinks[i], swa_q_norm_w[i], swa_k_norm_w[i])
            hf = _matmul_res(att.reshape(b * lp, -1), swa_wo[i].astype(BF16), hf, tm=1024)
        h = _ffn(hf.reshape(b, lp, d), ffn_norm_w[layer], ffn_w_up[layer], ffn_conv_w[layer],
                 ffn_w_down[layer])
    return h[:, ROW0:]
```

```python
import functools

import jax
import jax.numpy as jnp
from jax import lax
from jax.experimental import pallas as pl
from jax.experimental.pallas import tpu as pltpu

F32 = jnp.float32
BF16 = jnp.bfloat16

D_MODEL = 1024
N_META = 16
D_CONV = 512
DN_HEADS = 4
DN_HEAD_DIM = 128
DN_DIM = DN_HEADS * DN_HEAD_DIM
DN_CHUNK = 64
SWA_HEADS = 16
SWA_KV_HEADS = 4
SWA_GROUP = SWA_HEADS // SWA_KV_HEADS
SWA_HEAD_DIM = 64
SWA_BLOCK = 128
D_FF = 2816
EPS = 1e-6

LANES = 128
ROW0 = 128
PAD_ROWS = ROW0 - N_META
P_MAIN = 3 * D_CONV + 4 * DN_DIM
NEG = -1e30
VMEM_LIMIT = 56 * 1024 * 1024


def _cparams(sem):
    return pltpu.CompilerParams(dimension_semantics=sem, vmem_limit_bytes=VMEM_LIMIT)


def _rms(x, w):
    ms = jnp.mean(x * x, axis=-1, keepdims=True)
    return x * lax.rsqrt(ms + EPS) * w


def _sigmoid(x):
    return 1.0 / (1.0 + jnp.exp(-x))


def _silu(x):
    return x * _sigmoid(x)


def _dot(a, b):
    return jnp.dot(a, b, preferred_element_type=F32)


def _dot_nt(a, b):
    return lax.dot_general(a, b, (((1,), (1,)), ((), ())), preferred_element_type=F32)


def _dot_tn(a, b):
    return lax.dot_general(a, b, (((0,), (0,)), ((), ())), preferred_element_type=F32)


def _norm_matmul_kernel(x_ref, nw_ref, w_ref, o_ref, xn_ref):
    @pl.when(pl.program_id(1) == 0)
    def _():
        xn_ref[...] = _rms(x_ref[...], nw_ref[...]).astype(BF16)

    o_ref[...] = _dot(xn_ref[...], w_ref[...]).astype(o_ref.dtype)


def _norm_matmul(x, nw, w, *, tm, tn, out_dtype=BF16):
    m, k = x.shape
    n = w.shape[1]
    return pl.pallas_call(
        _norm_matmul_kernel,
        grid=(m // tm, n // tn),
        in_specs=[pl.BlockSpec((tm, k), lambda i, j: (i, 0)),
                  pl.BlockSpec((1, k), lambda i, j: (0, 0)),
                  pl.BlockSpec((k, tn), lambda i, j: (0, j))],
        out_specs=pl.BlockSpec((tm, tn), lambda i, j: (i, j)),
        out_shape=jax.ShapeDtypeStruct((m, n), out_dtype),
        scratch_shapes=[pltpu.VMEM((tm, k), BF16)],
        compiler_params=_cparams(("parallel", "arbitrary")),
        name="norm_matmul",
    )(x, nw, w)


def _in_proj_kernel(x_ref, nw_ref, w_ref, wg_ref, o_ref, og_ref, xn_ref):
    @pl.when(pl.program_id(1) == 0)
    def _():
        xn = _rms(x_ref[...], nw_ref[...]).astype(BF16)
        xn_ref[...] = xn
        og_ref[...] = _dot(xn, wg_ref[...])

    o_ref[...] = _dot(xn_ref[...], w_ref[...]).astype(o_ref.dtype)


def _in_proj(x, nw, w, wg, *, tm, tn):
    m, k = x.shape
    n = w.shape[1]
    return pl.pallas_call(
        _in_proj_kernel,
        grid=(m // tm, n // tn),
        in_specs=[pl.BlockSpec((tm, k), lambda i, j: (i, 0)),
                  pl.BlockSpec((1, k), lambda i, j: (0, 0)),
                  pl.BlockSpec((k, tn), lambda i, j: (0, j)),
                  pl.BlockSpec((k, LANES), lambda i, j: (0, 0))],
        out_specs=[pl.BlockSpec((tm, tn), lambda i, j: (i, j)),
                   pl.BlockSpec((tm, LANES), lambda i, j: (i, 0))],
        out_shape=[jax.ShapeDtypeStruct((m, n), BF16),
                   jax.ShapeDtypeStruct((m, LANES), F32)],
        scratch_shapes=[pltpu.VMEM((tm, k), BF16)],
        compiler_params=_cparams(("parallel", "arbitrary")),
        name="in_proj",
    )(x, nw, w, wg)


def _matmul_res_kernel(x_ref, w_ref, r_ref, o_ref):
    o_ref[...] = r_ref[...] + _dot(x_ref[...], w_ref[...])


def _matmul_res(x, w, res, *, tm):
    m, k = x.shape
    n = w.shape[1]
    return pl.pallas_call(
        _matmul_res_kernel,
        grid=(m // tm,),
        in_specs=[pl.BlockSpec((tm, k), lambda i: (i, 0)),
                  pl.BlockSpec((k, n), lambda i: (0, 0)),
                  pl.BlockSpec((tm, n), lambda i: (i, 0))],
        out_specs=pl.BlockSpec((tm, n), lambda i: (i, 0)),
        out_shape=jax.ShapeDtypeStruct((m, n), F32),
        input_output_aliases={2: 0},
        compiler_params=_cparams(("parallel",)),
        name="matmul_res",
    )(x, w, res)


def _ffn_up_kernel(x_ref, nw_ref, wg_ref, wv_ref, cw_ref, o_ref, xn_ref, gs_ref, carry_ref):
    t = pl.program_id(1)
    j = pl.program_id(2)
    tm = x_ref.shape[1]

    @pl.when(j == 0)
    def _():
        xn_ref[...] = _rms(x_ref[0], nw_ref[...]).astype(BF16)

    xn = xn_ref[...]
    gate = _dot(xn, wg_ref[...])
    val = _dot(xn, wv_ref[...])

    @pl.when(t == 0)
    def _():
        gs_ref[0:8, :] = jnp.zeros((8, gs_ref.shape[1]), F32)

    @pl.when(t > 0)
    def _():
        gs_ref[0:8, :] = carry_ref[j]

    gs_ref[8:8 + tm, :] = gate
    conv = (cw_ref[0:1, :] * gs_ref[6:6 + tm, :]
            + cw_ref[1:2, :] * gs_ref[7:7 + tm, :]
            + cw_ref[2:3, :] * gate)
    carry_ref[j] = gs_ref[tm:tm + 8, :]
    o_ref[0] = (_silu(conv) * val).astype(o_ref.dtype)


def _ffn_up(h, nw, wg, wv, cw, *, tm, tf):
    b, lp, d = h.shape
    f = wg.shape[1]
    nf = f // tf
    return pl.pallas_call(
        _ffn_up_kernel,
        grid=(b, lp // tm, nf),
        in_specs=[pl.BlockSpec((1, tm, d), lambda bi, t, j: (bi, t, 0)),
                  pl.BlockSpec((1, d), lambda bi, t, j: (0, 0)),
                  pl.BlockSpec((d, tf), lambda bi, t, j: (0, j)),
                  pl.BlockSpec((d, tf), lambda bi, t, j: (0, j)),
                  pl.BlockSpec((3, tf), lambda bi, t, j: (0, j))],
        out_specs=pl.BlockSpec((1, tm, tf), lambda bi, t, j: (bi, t, j)),
        out_shape=jax.ShapeDtypeStruct((b, lp, f), BF16),
        scratch_shapes=[pltpu.VMEM((tm, d), BF16),
                        pltpu.VMEM((tm + 8, tf), F32),
                        pltpu.VMEM((nf, 8, tf), F32)],
        compiler_params=_cparams(("arbitrary", "arbitrary", "arbitrary")),
        name="ffn_up",
    )(h, nw, wg, wv, cw)


MIX_PAIR = 2 * DN_CHUNK
MIX_PAIRS = 3
CONV_CARRY = 16
MIX_ROWS = MIX_PAIR * MIX_PAIRS


def _mixer_kernel(p_ref, pa_ref, pq_ref, gt_ref, sel_ref, shift_ref, conva_ref, dnconv_ref, alog_ref,
                  dtb_ref, dnw_ref, o_ref, xa_ref, s_ref):
    r = MIX_ROWS
    pr = MIX_PAIR
    c = DN_CHUNK
    dh = DN_HEAD_DIM
    t = pl.program_id(1)
    has_prev = t > 0

    @pl.when(t == 0)
    def _():
        s_ref[...] = jnp.zeros(s_ref.shape, F32)

    prev_a = (pa_ref[0, CONV_CARRY - 8:CONV_CARRY, 0:D_CONV].astype(F32)
              * pa_ref[0, CONV_CARRY - 8:CONV_CARRY, 2 * D_CONV:3 * D_CONV].astype(F32))
    xa_ref[0:8, :] = jnp.where(has_prev, prev_a, 0.0)
    xa_ref[8:8 + r, :] = (p_ref[0, :, 0:D_CONV].astype(F32)
                          * p_ref[0, :, 2 * D_CONV:3 * D_CONV].astype(F32))
    ca = (conva_ref[0:1, :] * xa_ref[6:6 + r, :]
          + conva_ref[1:2, :] * xa_ref[7:7 + r, :]
          + conva_ref[2:3, :] * xa_ref[8:8 + r, :])
    o_ref[0, :, 0:D_CONV] = (p_ref[0, :, D_CONV:2 * D_CONV].astype(F32) * ca).astype(o_ref.dtype)

    q0 = 3 * D_CONV
    xq = p_ref[0, :, q0:q0 + 3 * DN_DIM]
    prev_q = jnp.where(has_prev, pq_ref[0], jnp.zeros((CONV_CARRY, 3 * DN_DIM), BF16))
    taps = _dot(shift_ref[...], jnp.concatenate([prev_q, xq], axis=0))
    cq = dnconv_ref[3:4, :] * xq.astype(F32)
    for j in range(3):
        cq = cq + dnconv_ref[j:j + 1, :] * taps[j * r:(j + 1) * r]
    qkv = _silu(cq)

    gt = gt_ref[0]
    row = t * r + lax.broadcasted_iota(jnp.int32, (r, LANES), 0)
    live = row >= PAD_ROWS
    beta_all = jnp.where(live, _sigmoid(gt), 0.0)
    xg = gt + dtb_ref[...]
    softplus = jnp.maximum(xg, 0.0) + jnp.log1p(jnp.exp(-jnp.abs(xg)))
    g_all = jnp.where(live, -jnp.exp(alog_ref[...]) * softplus, 0.0)
    g_hi = g_all.astype(BF16)
    g_r1 = g_all - g_hi.astype(F32)
    g_mid = g_r1.astype(BF16)
    g_lo = (g_r1 - g_mid.astype(F32)).astype(BF16)
    sel = sel_ref[...]

    ii = lax.broadcasted_iota(jnp.int32, (pr, pr), 0)
    jj = lax.broadcasted_iota(jnp.int32, (pr, pr), 1)
    same = (ii >= c) == (jj >= c)
    incl = same & (ii >= jj)
    strict = same & (ii > jj)
    first = lax.broadcasted_iota(jnp.int32, (pr, 1), 0) < c

    keys = [(pi, h) for pi in range(MIX_PAIRS) for h in range(DN_HEADS)]
    local = {}
    amat = {}
    for pi in range(MIX_PAIRS):
        ps = slice(pi * pr, (pi + 1) * pr)
        cums = _dot(sel, g_hi[ps]) + (_dot(sel, g_mid[ps]) + _dot(sel, g_lo[ps]))
        cdec = cums[0:pr]
        dlast = tuple(jnp.broadcast_to(cums[pr + 8 * ci:pr + 8 * ci + 1], (pr, LANES)) for ci in range(2))
        cdec_t = cdec.T
        for h in range(DN_HEADS):
            q = qkv[ps, h * dh:(h + 1) * dh]
            k = qkv[ps, DN_DIM + h * dh:DN_DIM + (h + 1) * dh]
            v = qkv[ps, 2 * DN_DIM + h * dh:2 * DN_DIM + (h + 1) * dh]
            q = q * lax.rsqrt(jnp.sum(q * q, -1, keepdims=True) + EPS) * (dh ** -0.5)
            k = k * lax.rsqrt(jnp.sum(k * k, -1, keepdims=True) + EPS)
            gl = DN_HEADS + h
            beta = beta_all[ps, h:h + 1]
            cd_col = cdec[:, gl:gl + 1]
            cd_row = cdec_t[gl:gl + 1, :]
            dmask = jnp.where(incl, jnp.exp(jnp.where(incl, cd_col - cd_row, 0.0)), 0.0)
            kb = k.astype(BF16)
            qk_kk = _dot_nt(jnp.concatenate([q.astype(BF16), kb], axis=0), kb)
            amat[pi, h] = jnp.where(strict, beta * qk_kk[pr:2 * pr] * dmask, 0.0)
            e_cd = jnp.exp(cd_col)
            dl_own = jnp.where(first, dlast[0][:, gl:gl + 1], dlast[1][:, gl:gl + 1])
            local[pi, h] = dict(
                rhs=jnp.concatenate([v * beta, k * (beta * e_cd)], axis=1),
                qk=(qk_kk[0:pr] * dmask).astype(BF16),
                q_dec=(q * e_cd).astype(BF16),
                k_dec_t=(k * jnp.exp(dl_own - cd_col)).T.astype(BF16),
                g_last=[jnp.exp(d[:, gl:gl + 1]) for d in dlast])

    tm = {key: -amat[key] for key in keys}
    pw = dict(amat)
    for _ in range(5):
        for key in keys:
            pwb = pw[key].astype(BF16)
            pw[key] = _dot(pwb, pwb)
        for key in keys:
            tm[key] = tm[key] + pw[key] + _dot(tm[key].astype(BF16), pw[key].astype(BF16))
    for key in keys:
        rhs = local[key]["rhs"]
        sol = rhs + _dot(tm[key].astype(BF16), rhs.astype(BF16))
        local[key]["u"] = sol[:, 0:dh]
        local[key]["w"] = sol[:, dh:2 * dh].astype(BF16)

    z0 = 3 * D_CONV + 3 * DN_DIM
    state = [s_ref[h] for h in range(DN_HEADS)]
    outs = [[] for _ in range(DN_HEADS)]
    zeros = jnp.zeros((c, dh), BF16)
    for pi in range(MIX_PAIRS):
        for ci in range(2):
            rs = slice(ci * c, (ci + 1) * c)
            ws_qs = [_dot(jnp.concatenate([local[pi, h]["w"][rs], local[pi, h]["q_dec"][rs]], axis=0),
                          state[h].astype(BF16)) for h in range(DN_HEADS)]
            v_full = []
            for h in range(DN_HEADS):
                vb = (local[pi, h]["u"][rs] - ws_qs[h][0:c]).astype(BF16)
                v_full.append(jnp.concatenate([vb, zeros] if ci == 0 else [zeros, vb], axis=0))
            for h in range(DN_HEADS):
                state[h] = (state[h] * local[pi, h]["g_last"][ci]
                            + _dot(local[pi, h]["k_dec_t"], v_full[h]))
            for h in range(DN_HEADS):
                outs[h].append(ws_qs[h][c:2 * c] + _dot(local[pi, h]["qk"][rs], v_full[h]))
    for h in range(DN_HEADS):
        s_ref[h] = state[h]
        o = jnp.concatenate(outs[h], axis=0)
        z = p_ref[0, :, z0 + h * dh:z0 + (h + 1) * dh].astype(F32)
        y = _rms(o, dnw_ref[...]) * _silu(z)
        o_ref[0, :, D_CONV + h * dh:D_CONV + (h + 1) * dh] = y.astype(o_ref.dtype)


def _decay_selectors():
    r, c = MIX_PAIR, DN_CHUNK
    i = jnp.arange(r)[:, None]
    j = jnp.arange(r)[None, :]
    tri = ((i // c) == (j // c)) & (i >= j)
    last0 = jnp.broadcast_to(j < c, (8, r))
    last1 = jnp.broadcast_to(j >= c, (8, r))
    return jnp.concatenate([tri, last0, last1], axis=0).astype(BF16)


def _shift_matrix(rows):
    t = jnp.arange(rows)[:, None]
    s = jnp.arange(CONV_CARRY + rows)[None, :]
    return jnp.concatenate([s == t + CONV_CARRY - d for d in (3, 2, 1)], axis=0).astype(BF16)


def _mixer(p, gt, conv_a_w, dn_conv_w, a_log, dt_bias, dn_norm_w):
    b, lp, _ = p.shape
    r = MIX_ROWS
    pad_lanes = lambda vec: jnp.zeros((1, LANES), F32).at[0, DN_HEADS:2 * DN_HEADS].set(vec.astype(F32))
    const = lambda shape: pl.BlockSpec(shape, lambda bi, t: (0,) * len(shape))
    prev_rows = lambda t: jnp.maximum(t * (r // CONV_CARRY) - 1, 0)
    qkv_cols = 3 * DN_DIM
    return pl.pallas_call(
        _mixer_kernel,
        grid=(b, lp // r),
        in_specs=[pl.BlockSpec((1, r, P_MAIN), lambda bi, t: (bi, t, 0)),
                  pl.BlockSpec((1, CONV_CARRY, qkv_cols), lambda bi, t: (bi, prev_rows(t), 0)),
                  pl.BlockSpec((1, CONV_CARRY, qkv_cols), lambda bi, t: (bi, prev_rows(t), 1)),
                  pl.BlockSpec((1, r, LANES), lambda bi, t: (bi, t, 0)),
                  const((MIX_PAIR + 16, MIX_PAIR)),
                  const((3 * r, CONV_CARRY + r)),
                  const((3, D_CONV)),
                  const((4, 3 * DN_DIM)),
                  const((1, LANES)),
                  const((1, LANES)),
                  const((1, DN_HEAD_DIM))],
        out_specs=pl.BlockSpec((1, r, D_CONV + DN_DIM), lambda bi, t: (bi, t, 0)),
        out_shape=jax.ShapeDtypeStruct((b, lp, D_CONV + DN_DIM), BF16),
        scratch_shapes=[pltpu.VMEM((r + 8, D_CONV), F32),
                        pltpu.VMEM((DN_HEADS, DN_HEAD_DIM, DN_HEAD_DIM), F32)],
        compiler_params=_cparams(("arbitrary", "arbitrary")),
        name="gdn_mixer",
    )(p, p, p, gt, _decay_selectors(), _shift_matrix(r), conv_a_w.astype(F32), dn_conv_w.astype(F32),
      pad_lanes(a_log), pad_lanes(dt_bias), dn_norm_w.astype(F32).reshape(1, DN_HEAD_DIM))


LOG2E = 1.4426950408889634
SWA_HALF = SWA_BLOCK // 2
SWA_KEYS = N_META + SWA_BLOCK + SWA_HALF


def _qkv_proj_kernel(x_ref, nw_ref, w_ref, hw_ref, o_ref):
    xn = _rms(x_ref[...], nw_ref[...]).astype(BF16)
    n_norm = hw_ref.shape[1]
    lo = lax.broadcasted_iota(jnp.int32, (1, LANES), 1) < SWA_HEAD_DIM
    for j in range(w_ref.shape[1] // LANES):
        cs = slice(j * LANES, (j + 1) * LANES)
        y = _dot(xn, w_ref[:, cs])
        if j * LANES < n_norm:
            y2 = y * y
            ms_lo = jnp.sum(jnp.where(lo, y2, 0.0), -1, keepdims=True) * (1.0 / SWA_HEAD_DIM)
            ms_hi = jnp.sum(jnp.where(lo, 0.0, y2), -1, keepdims=True) * (1.0 / SWA_HEAD_DIM)
            y = y * jnp.where(lo, lax.rsqrt(ms_lo + EPS), lax.rsqrt(ms_hi + EPS)) * hw_ref[:, cs]
        o_ref[:, cs] = y.astype(o_ref.dtype)


def _qkv_proj(x, nw, w, hw, *, tm):
    m, k = x.shape
    n = w.shape[1]
    return pl.pallas_call(
        _qkv_proj_kernel,
        grid=(m // tm,),
        in_specs=[pl.BlockSpec((tm, k), lambda i: (i, 0)),
                  pl.BlockSpec((1, k), lambda i: (0, 0)),
                  pl.BlockSpec((k, n), lambda i: (0, 0)),
                  pl.BlockSpec((1, hw.shape[1]), lambda i: (0, 0))],
        out_specs=pl.BlockSpec((tm, n), lambda i: (i, 0)),
        out_shape=jax.ShapeDtypeStruct((m, n), BF16),
        compiler_params=_cparams(("parallel",)),
        name="qkv_proj",
    )(x, nw, w, hw)


def _swa_kernel(sink_ref, q_ref, km_ref, kp_ref, kc_ref, vm_ref, vp_ref, vc_ref, o_ref):
    i = pl.program_id(1)
    hb = SWA_HALF
    nk = SWA_KEYS
    lo = lax.broadcasted_iota(jnp.int32, (1, LANES), 1) < SWA_HEAD_DIM

    bias = []
    for hf in range(2):
        r = hf * hb + lax.broadcasted_iota(jnp.int32, (hb, nk), 0)
        cidx = lax.broadcasted_iota(jnp.int32, (hb, nk), 1)
        n_prev = SWA_BLOCK - hf * hb
        prev_j = cidx - N_META + hf * hb
        cur_j = cidx - N_META - n_prev
        meta_ok = (cidx < N_META) & ((i > 0) | (cidx <= r - PAD_ROWS))
        prev_ok = (cidx >= N_META) & (cidx < N_META + n_prev) & (prev_j > r) & (i >= 2)
        cur_ok = (cidx >= N_META + n_prev) & (cur_j <= r) & (i >= 1)
        bh = jnp.where(meta_ok | prev_ok | cur_ok, 0.0, NEG)
        bias.append(jnp.concatenate([bh] * SWA_GROUP, axis=0))

    def windows(m_ref, p_ref, c_ref, slab):
        cs = slice(slab * LANES, (slab + 1) * LANES)
        full = jnp.concatenate([m_ref[0, :, cs], p_ref[0, :, cs], c_ref[0, :, cs]], axis=0)
        swapped = pltpu.roll(full, SWA_HEAD_DIM, 1)
        return full, swapped

    def split(full, swapped, parity):
        zero = jnp.zeros_like(full)
        if parity == 0:
            return jnp.where(lo, full, zero), jnp.where(lo, zero, swapped)
        return jnp.where(lo, swapped, zero), jnp.where(lo, zero, full)

    def window(x, hf):
        if hf == 0:
            return x[0:nk]
        return jnp.concatenate([x[0:N_META], x[N_META + hb:N_META + 2 * SWA_BLOCK]], axis=0)

    kslabs = [windows(km_ref, kp_ref, kc_ref, s) for s in range(SWA_KV_HEADS // 2)]
    vslabs = [windows(vm_ref, vp_ref, vc_ref, s) for s in range(SWA_KV_HEADS // 2)]

    problems = [(kv, hf) for kv in range(SWA_KV_HEADS) for hf in range(2)]

    def logits(kv, hf):
        rows = slice(hf * hb, (hf + 1) * hb)
        base = kv * SWA_GROUP * SWA_HEAD_DIM
        lhs = jnp.concatenate([q_ref[0, rows, base:base + LANES],
                               q_ref[0, rows, base + LANES:base + 2 * LANES]], axis=0)
        k_lo, k_hi = split(*kslabs[kv // 2], kv % 2)
        return jnp.concatenate([_dot_nt(lhs, window(k_lo, hf)), _dot_nt(lhs, window(k_hi, hf))],
                               axis=0) + bias[hf]

    grp = lax.broadcasted_iota(jnp.int32, (SWA_GROUP * hb, 1), 0) // hb
    nxt = logits(*problems[0])
    for n, (kv, hf) in enumerate(problems):
        lg = nxt
        if n + 1 < len(problems):
            nxt = logits(*problems[n + 1])
        sink = jnp.zeros((SWA_GROUP * hb, 1), F32)
        for gi, g in enumerate((0, 2, 1, 3)):
            sink = jnp.where(grp == gi, sink_ref[kv * SWA_GROUP + g], sink)
        m = jnp.maximum(jnp.max(lg, axis=-1, keepdims=True), sink)
        e = jnp.exp2(lg - m)
        den = jnp.sum(e, axis=-1, keepdims=True) + jnp.exp2(sink - m)
        p = e.astype(BF16)
        v_lo, v_hi = split(*vslabs[kv // 2], kv % 2)
        o = _dot(p[0:2 * hb], window(v_lo, hf)) + _dot(p[2 * hb:4 * hb], window(v_hi, hf))
        rden = 1.0 / den
        o = o * jnp.where(lo, rden[0:2 * hb], rden[2 * hb:4 * hb])
        rows = slice(hf * hb, (hf + 1) * hb)
        base = kv * SWA_GROUP * SWA_HEAD_DIM
        o_ref[0, rows, base:base + LANES] = o[0:hb].astype(o_ref.dtype)
        o_ref[0, rows, base + LANES:base + 2 * LANES] = o[hb:2 * hb].astype(o_ref.dtype)


def _swa(qkv, sinks):
    b, lp, _ = qkv.shape
    blk = SWA_BLOCK
    nq = SWA_HEADS * SWA_HEAD_DIM
    nkv = SWA_KV_HEADS * SWA_HEAD_DIM
    kcol = nq // nkv
    vcol = kcol + 1
    meta_blk = PAD_ROWS // N_META
    prev = lambda t: jnp.maximum(t - 1, 0)
    return pl.pallas_call(
        _swa_kernel,
        grid=(b, lp // blk),
        in_specs=[pl.BlockSpec(memory_space=pltpu.SMEM),
                  pl.BlockSpec((1, blk, nq), lambda bi, t: (bi, t, 0)),
                  pl.BlockSpec((1, N_META, nkv), lambda bi, t: (bi, meta_blk, kcol)),
                  pl.BlockSpec((1, blk, nkv), lambda bi, t: (bi, prev(t), kcol)),
                  pl.BlockSpec((1, blk, nkv), lambda bi, t: (bi, t, kcol)),
                  pl.BlockSpec((1, N_META, nkv), lambda bi, t: (bi, meta_blk, vcol)),
                  pl.BlockSpec((1, blk, nkv), lambda bi, t: (bi, prev(t), vcol)),
                  pl.BlockSpec((1, blk, nkv), lambda bi, t: (bi, t, vcol))],
        out_specs=pl.BlockSpec((1, blk, nq), lambda bi, t: (bi, t, 0)),
        out_shape=jax.ShapeDtypeStruct((b, lp, nq), BF16),
        compiler_params=_cparams(("parallel", "arbitrary")),
        name="swa",
    )((sinks.astype(F32) * LOG2E), qkv, qkv, qkv, qkv, qkv, qkv, qkv)


def _ffn(h, nw, w_up, conv_w, w_down):
    b, lp, d = h.shape
    wg = w_up[:, :D_FF].astype(BF16)
    wv = w_up[:, D_FF:].astype(BF16)
    act = _ffn_up(h, nw.reshape(1, d), wg, wv, conv_w.astype(F32), tm=704, tf=1408)
    out = _matmul_res(act.reshape(b * lp, D_FF), w_down.astype(BF16), h.reshape(b * lp, d), tm=512)
    return out.reshape(b, lp, d)


def kernel(x, meta_tokens, attn_norm_w, ffn_norm_w, mix_w_in, conv_a_w, dn_conv_w, dn_a_log,
           dn_dt_bias, dn_norm_w, mix_w_out, swa_wq, swa_wk, swa_wv, swa_q_norm_w, swa_k_norm_w,
           swa_sinks, swa_wo, ffn_w_up, ffn_conv_w, ffn_w_down):
    b, seq, d = x.shape
    lp = ROW0 + seq
    meta = jnp.broadcast_to(meta_tokens[None].astype(x.dtype), (b, N_META, d))
    h = jnp.concatenate([jnp.zeros((b, PAD_ROWS, d), x.dtype), meta, x], axis=1)
    depth = attn_norm_w.shape[0]
    for layer in range(depth):
        i = layer // 2
        nw = attn_norm_w[layer].reshape(1, d)
        hf = h.reshape(b * lp, d)
        if layer % 2 == 0:
            w_in = mix_w_in[i]
            wg = jnp.zeros((d, LANES), BF16).at[:, :2 * DN_HEADS].set(w_in[:, P_MAIN:].astype(BF16))
            p, gt = _in_proj(hf, nw, w_in[:, :P_MAIN].astype(BF16), wg, tm=1024, tn=512)
            y = _mixer(p.reshape(b, lp, P_MAIN), gt.reshape(b, lp, LANES), conv_a_w[i], dn_conv_w[i],
                       dn_a_log[i], dn_dt_bias[i], dn_norm_w[i])
            hf = _matmul_res(y.reshape(b * lp, -1), mix_w_out[i].astype(BF16), hf, tm=1024)
        else:
            wqkv = jnp.concatenate([swa_wq[i], swa_wk[i], swa_wv[i]], axis=1).astype(BF16)
            head_w = jnp.concatenate([
                jnp.tile(swa_q_norm_w[i].astype(F32) * (SWA_HEAD_DIM ** -0.5 * LOG2E), SWA_HEADS),
                jnp.tile(swa_k_norm_w[i].astype(F32), SWA_KV_HEADS)]).reshape(1, -1)
            qkv = _qkv_proj(hf, nw, wqkv, head_w, tm=512)
            att = _swa(qkv.reshape(b, lp, -1), swa_sinks[i])
            hf = _matmul_res(att.reshape(b * lp, -1), swa_wo[i].astype(BF16), hf, tm=1024)
        h = _ffn(hf.reshape(b, lp, d), ffn_norm_w[layer], ffn_w_up[layer], ffn_conv_w[layer],
                 ffn_w_down[layer])
    return h[:, ROW0:]
```

```python
import functools

import jax
import jax.numpy as jnp
from jax import lax
from jax.experimental import pallas as pl
from jax.experimental.pallas import tpu as pltpu

F32 = jnp.float32
BF16 = jnp.bfloat16

D_MODEL = 1024
N_META = 16
D_CONV = 512
DN_HEADS = 4
DN_HEAD_DIM = 128
DN_DIM = DN_HEADS * DN_HEAD_DIM
DN_CHUNK = 64
SWA_HEADS = 16
SWA_KV_HEADS = 4
SWA_GROUP = SWA_HEADS // SWA_KV_HEADS
SWA_HEAD_DIM = 64
SWA_BLOCK = 128
D_FF = 2816
EPS = 1e-6

LANES = 128
ROW0 = 128
PAD_ROWS = ROW0 - N_META
P_MAIN = 3 * D_CONV + 4 * DN_DIM
NEG = -1e30
VMEM_LIMIT = 56 * 1024 * 1024


def _cparams(sem):
    return pltpu.CompilerParams(dimension_semantics=sem, vmem_limit_bytes=VMEM_LIMIT)


def _rms(x, w):
    ms = jnp.mean(x * x, axis=-1, keepdims=True)
    return x * lax.rsqrt(ms + EPS) * w


def _sigmoid(x):
    return 1.0 / (1.0 + jnp.exp(-x))


def _silu(x):
    return x * _sigmoid(x)


def _dot(a, b):
    return jnp.dot(a, b, preferred_element_type=F32)


def _dot_nt(a, b):
    return lax.dot_general(a, b, (((1,), (1,)), ((), ())), preferred_element_type=F32)


def _dot_tn(a, b):
    return lax.dot_general(a, b, (((0,), (0,)), ((), ())), preferred_element_type=F32)


def _norm_matmul_kernel(x_ref, nw_ref, w_ref, o_ref, xn_ref):
    @pl.when(pl.program_id(1) == 0)
    def _():
        xn_ref[...] = _rms(x_ref[...], nw_ref[...]).astype(BF16)

    o_ref[...] = _dot(xn_ref[...], w_ref[...]).astype(o_ref.dtype)


def _norm_matmul(x, nw, w, *, tm, tn, out_dtype=BF16):
    m, k = x.shape
    n = w.shape[1]
    return pl.pallas_call(
        _norm_matmul_kernel,
        grid=(m // tm, n // tn),
        in_specs=[pl.BlockSpec((tm, k), lambda i, j: (i, 0)),
                  pl.BlockSpec((1, k), lambda i, j: (0, 0)),
                  pl.BlockSpec((k, tn), lambda i, j: (0, j))],
        out_specs=pl.BlockSpec((tm, tn), lambda i, j: (i, j)),
        out_shape=jax.ShapeDtypeStruct((m, n), out_dtype),
        scratch_shapes=[pltpu.VMEM((tm, k), BF16)],
        compiler_params=_cparams(("parallel", "arbitrary")),
        name="norm_matmul",
    )(x, nw, w)


def _in_proj_kernel(x_ref, nw_ref, w_ref, wg_ref, o_ref, og_ref, xn_ref):
    @pl.when(pl.program_id(1) == 0)
    def _():
        xn = _rms(x_ref[...], nw_ref[...]).astype(BF16)
        xn_ref[...] = xn
        og_ref[...] = _dot(xn, wg_ref[...])

    o_ref[...] = _dot(xn_ref[...], w_ref[...]).astype(o_ref.dtype)


def _in_proj(x, nw, w, wg, *, tm, tn):
    m, k = x.shape
    n = w.shape[1]
    return pl.pallas_call(
        _in_proj_kernel,
        grid=(m // tm, n // tn),
        in_specs=[pl.BlockSpec((tm, k), lambda i, j: (i, 0)),
                  pl.BlockSpec((1, k), lambda i, j: (0, 0)),
                  pl.BlockSpec((k, tn), lambda i, j: (0, j)),
                  pl.BlockSpec((k, LANES), lambda i, j: (0, 0))],
        out_specs=[pl.BlockSpec((tm, tn), lambda i, j: (i, j)),
                   pl.BlockSpec((tm, LANES), lambda i, j: (i, 0))],
        out_shape=[jax.ShapeDtypeStruct((m, n), BF16),
                   jax.ShapeDtypeStruct((m, LANES), F32)],
        scratch_shapes=[pltpu.VMEM((tm, k), BF16)],
        compiler_params=_cparams(("parallel", "arbitrary")),
        name="in_proj",
    )(x, nw, w, wg)


def _matmul_res_kernel(x_ref, w_ref, r_ref, o_ref):
    o_ref[...] = r_ref[...] + _dot(x_ref[...], w_ref[...])


def _matmul_res(x, w, res, *, tm):
    m, k = x.shape
    n = w.shape[1]
    return pl.pallas_call(
        _matmul_res_kernel,
        grid=(m // tm,),
        in_specs=[pl.BlockSpec((tm, k), lambda i: (i, 0)),
                  pl.BlockSpec((k, n), lambda i: (0, 0)),
                  pl.BlockSpec((tm, n), lambda i: (i, 0))],
        out_specs=pl.BlockSpec((tm, n), lambda i: (i, 0)),
        out_shape=jax.ShapeDtypeStruct((m, n), F32),
        input_output_aliases={2: 0},
        compiler_params=_cparams(("parallel",)),
        name="matmul_res",
    )(x, w, res)


FFN_HALO = 16
FFN_CHUNK = 256


def _ffn_kernel(x_ref, xp_ref, nw_ref, wg_ref, wv_ref, cw_ref, wd_ref, o_ref, act_ref):
    t = pl.program_id(1)
    tm = x_ref.shape[1]
    x = x_ref[0]
    xn = _rms(x, nw_ref[...]).astype(BF16)
    xp = jnp.where(t > 0, xp_ref[0], 0.0)
    xn_ext = jnp.concatenate([_rms(xp, nw_ref[...]).astype(BF16), xn], axis=0)
    for j in range(act_ref.shape[1] // FFN_CHUNK):
        cs = slice(j * FFN_CHUNK, (j + 1) * FFN_CHUNK)
        g = _dot(xn_ext, wg_ref[:, cs])
        v = _dot(xn, wv_ref[:, cs])
        g1 = pltpu.roll(g, 1, 0)[FFN_HALO:]
        g2 = pltpu.roll(g, 2, 0)[FFN_HALO:]
        conv = cw_ref[0:1, cs] * g2 + cw_ref[1:2, cs] * g1 + cw_ref[2:3, cs] * g[FFN_HALO:]
        act_ref[:, cs] = (_silu(conv) * v).astype(BF16)
    o_ref[0] = x + _dot(act_ref[...], wd_ref[...])


def _ffn(h, nw, w_up, conv_w, w_down, *, tm):
    b, lp, d = h.shape
    f = w_down.shape[0]
    resident = lambda shape: pl.BlockSpec(shape, lambda bi, t: (0,) * len(shape),
                                          pipeline_mode=pl.Buffered(1))
    halo_blk = lambda t: jnp.maximum(t * (tm // FFN_HALO) - 1, 0)
    return pl.pallas_call(
        _ffn_kernel,
        grid=(b, lp // tm),
        in_specs=[pl.BlockSpec((1, tm, d), lambda bi, t: (bi, t, 0)),
                  pl.BlockSpec((1, FFN_HALO, d), lambda bi, t: (bi, halo_blk(t), 0)),
                  resident((1, d)),
                  resident((d, f)),
                  resident((d, f)),
                  resident((3, f)),
                  resident((f, d))],
        out_specs=pl.BlockSpec((1, tm, d), lambda bi, t: (bi, t, 0)),
        out_shape=jax.ShapeDtypeStruct((b, lp, d), F32),
        scratch_shapes=[pltpu.VMEM((tm, f), BF16)],
        compiler_params=_cparams(("parallel", "arbitrary")),
        name="ffn",
    )(h, h, nw.reshape(1, d), w_up[:, :f].astype(BF16), w_up[:, f:].astype(BF16),
      conv_w.astype(F32), w_down.astype(BF16))


MIX_PAIR = 2 * DN_CHUNK
MIX_PAIRS = 3
CONV_CARRY = 16
MIX_ROWS = MIX_PAIR * MIX_PAIRS


def _mixer_kernel(p_ref, pa_ref, pq_ref, gt_ref, sel_ref, shift_ref, conva_ref, dnconv_ref, alog_ref,
                  dtb_ref, dnw_ref, o_ref, xa_ref, s_ref):
    r = MIX_ROWS
    pr = MIX_PAIR
    c = DN_CHUNK
    dh = DN_HEAD_DIM
    t = pl.program_id(1)
    has_prev = t > 0

    @pl.when(t == 0)
    def _():
        s_ref[...] = jnp.zeros(s_ref.shape, F32)

    prev_a = (pa_ref[0, CONV_CARRY - 8:CONV_CARRY, 0:D_CONV].astype(F32)
              * pa_ref[0, CONV_CARRY - 8:CONV_CARRY, 2 * D_CONV:3 * D_CONV].astype(F32))
    xa_ref[0:8, :] = jnp.where(has_prev, prev_a, 0.0)
    xa_ref[8:8 + r, :] = (p_ref[0, :, 0:D_CONV].astype(F32)
                          * p_ref[0, :, 2 * D_CONV:3 * D_CONV].astype(F32))
    ca = (conva_ref[0:1, :] * xa_ref[6:6 + r, :]
          + conva_ref[1:2, :] * xa_ref[7:7 + r, :]
          + conva_ref[2:3, :] * xa_ref[8:8 + r, :])
    o_ref[0, :, 0:D_CONV] = (p_ref[0, :, D_CONV:2 * D_CONV].astype(F32) * ca).astype(o_ref.dtype)

    q0 = 3 * D_CONV
    xq = p_ref[0, :, q0:q0 + 3 * DN_DIM]
    prev_q = jnp.where(has_prev, pq_ref[0], jnp.zeros((CONV_CARRY, 3 * DN_DIM), BF16))
    taps = _dot(shift_ref[...], jnp.concatenate([prev_q, xq], axis=0))
    cq = dnconv_ref[3:4, :] * xq.astype(F32)
    for j in range(3):
        cq = cq + dnconv_ref[j:j + 1, :] * taps[j * r:(j + 1) * r]
    qkv = _silu(cq)

    gt = gt_ref[0]
    row = t * r + lax.broadcasted_iota(jnp.int32, (r, LANES), 0)
    live = row >= PAD_ROWS
    beta_all = jnp.where(live, _sigmoid(gt), 0.0)
    xg = gt + dtb_ref[...]
    softplus = jnp.maximum(xg, 0.0) + jnp.log1p(jnp.exp(-jnp.abs(xg)))
    g_all = jnp.where(live, -jnp.exp(alog_ref[...]) * softplus, 0.0)
    g_hi = g_all.astype(BF16)
    g_r1 = g_all - g_hi.astype(F32)
    g_mid = g_r1.astype(BF16)
    g_lo = (g_r1 - g_mid.astype(F32)).astype(BF16)
    sel = sel_ref[...]

    ii = lax.broadcasted_iota(jnp.int32, (pr, pr), 0)
    jj = lax.broadcasted_iota(jnp.int32, (pr, pr), 1)
    same = (ii >= c) == (jj >= c)
    incl = same & (ii >= jj)
    strict = same & (ii > jj)
    first = lax.broadcasted_iota(jnp.int32, (pr, 1), 0) < c

    keys = [(pi, h) for pi in range(MIX_PAIRS) for h in range(DN_HEADS)]
    local = {}
    amat = {}
    for pi in range(MIX_PAIRS):
        ps = slice(pi * pr, (pi + 1) * pr)
        cums = _dot(sel, g_hi[ps]) + (_dot(sel, g_mid[ps]) + _dot(sel, g_lo[ps]))
        cdec = cums[0:pr]
        dlast = tuple(jnp.broadcast_to(cums[pr + 8 * ci:pr + 8 * ci + 1], (pr, LANES)) for ci in range(2))
        cdec_t = cdec.T
        for h in range(DN_HEADS):
            q = qkv[ps, h * dh:(h + 1) * dh]
            k = qkv[ps, DN_DIM + h * dh:DN_DIM + (h + 1) * dh]
            v = qkv[ps, 2 * DN_DIM + h * dh:2 * DN_DIM + (h + 1) * dh]
            q = q * lax.rsqrt(jnp.sum(q * q, -1, keepdims=True) + EPS) * (dh ** -0.5)
            k = k * lax.rsqrt(jnp.sum(k * k, -1, keepdims=True) + EPS)
            gl = DN_HEADS + h
            beta = beta_all[ps, h:h + 1]
            cd_col = cdec[:, gl:gl + 1]
            cd_row = cdec_t[gl:gl + 1, :]
            dmask = jnp.where(incl, jnp.exp(jnp.where(incl, cd_col - cd_row, 0.0)), 0.0)
            kb = k.astype(BF16)
            qk_kk = _dot_nt(jnp.concatenate([q.astype(BF16), kb], axis=0), kb)
            amat[pi, h] = jnp.where(strict, beta * qk_kk[pr:2 * pr] * dmask, 0.0)
            e_cd = jnp.exp(cd_col)
            dl_own = jnp.where(first, dlast[0][:, gl:gl + 1], dlast[1][:, gl:gl + 1])
            local[pi, h] = dict(
                rhs=jnp.concatenate([v * beta, k * (beta * e_cd)], axis=1),
                qk=(qk_kk[0:pr] * dmask).astype(BF16),
                q_dec=(q * e_cd).astype(BF16),
                k_dec_t=(k * jnp.exp(dl_own - cd_col)).T.astype(BF16),
                g_last=[jnp.exp(d[:, gl:gl + 1]) for d in dlast])

    tm = {key: -amat[key] for key in keys}
    pw = dict(amat)
    for _ in range(5):
        for key in keys:
            pwb = pw[key].astype(BF16)
            pw[key] = _dot(pwb, pwb)
        for key in keys:
            tm[key] = tm[key] + pw[key] + _dot(tm[key].astype(BF16), pw[key].astype(BF16))
    for key in keys:
        rhs = local[key]["rhs"]
        sol = rhs + _dot(tm[key].astype(BF16), rhs.astype(BF16))
        local[key]["u"] = sol[:, 0:dh]
        local[key]["w"] = sol[:, dh:2 * dh].astype(BF16)

    z0 = 3 * D_CONV + 3 * DN_DIM
    state = [s_ref[h] for h in range(DN_HEADS)]
    outs = [[] for _ in range(DN_HEADS)]
    zeros = jnp.zeros((c, dh), BF16)
    for pi in range(MIX_PAIRS):
        for ci in range(2):
            rs = slice(ci * c, (ci + 1) * c)
            ws_qs = [_dot(jnp.concatenate([local[pi, h]["w"][rs], local[pi, h]["q_dec"][rs]], axis=0),
                          state[h].astype(BF16)) for h in range(DN_HEADS)]
            v_full = []
            for h in range(DN_HEADS):
                vb = (local[pi, h]["u"][rs] - ws_qs[h][0:c]).astype(BF16)
                v_full.append(jnp.concatenate([vb, zeros] if ci == 0 else [zeros, vb], axis=0))
            for h in range(DN_HEADS):
                state[h] = (state[h] * local[pi, h]["g_last"][ci]
                            + _dot(local[pi, h]["k_dec_t"], v_full[h]))
            for h in range(DN_HEADS):
                outs[h].append(ws_qs[h][c:2 * c] + _dot(local[pi, h]["qk"][rs], v_full[h]))
    for h in range(DN_HEADS):
        s_ref[h] = state[h]
        o = jnp.concatenate(outs[h], axis=0)
        z = p_ref[0, :, z0 + h * dh:z0 + (h + 1) * dh].astype(F32)
        y = _rms(o, dnw_ref[...]) * _silu(z)
        o_ref[0, :, D_CONV + h * dh:D_CONV + (h + 1) * dh] = y.astype(o_ref.dtype)


def _decay_selectors():
    r, c = MIX_PAIR, DN_CHUNK
    i = jnp.arange(r)[:, None]
    j = jnp.arange(r)[None, :]
    tri = ((i // c) == (j // c)) & (i >= j)
    last0 = jnp.broadcast_to(j < c, (8, r))
    last1 = jnp.broadcast_to(j >= c, (8, r))
    return jnp.concatenate([tri, last0, last1], axis=0).astype(BF16)


def _shift_matrix(rows):
    t = jnp.arange(rows)[:, None]
    s = jnp.arange(CONV_CARRY + rows)[None, :]
    return jnp.concatenate([s == t + CONV_CARRY - d for d in (3, 2, 1)], axis=0).astype(BF16)


def _mixer(p, gt, conv_a_w, dn_conv_w, a_log, dt_bias, dn_norm_w):
    b, lp, _ = p.shape
    r = MIX_ROWS
    pad_lanes = lambda vec: jnp.zeros((1, LANES), F32).at[0, DN_HEADS:2 * DN_HEADS].set(vec.astype(F32))
    const = lambda shape: pl.BlockSpec(shape, lambda bi, t: (0,) * len(shape))
    prev_rows = lambda t: jnp.maximum(t * (r // CONV_CARRY) - 1, 0)
    qkv_cols = 3 * DN_DIM
    return pl.pallas_call(
        _mixer_kernel,
        grid=(b, lp // r),
        in_specs=[pl.BlockSpec((1, r, P_MAIN), lambda bi, t: (bi, t, 0)),
                  pl.BlockSpec((1, CONV_CARRY, qkv_cols), lambda bi, t: (bi, prev_rows(t), 0)),
                  pl.BlockSpec((1, CONV_CARRY, qkv_cols), lambda bi, t: (bi, prev_rows(t), 1)),
                  pl.BlockSpec((1, r, LANES), lambda bi, t: (bi, t, 0)),
                  const((MIX_PAIR + 16, MIX_PAIR)),
                  const((3 * r, CONV_CARRY + r)),
                  const((3, D_CONV)),
                  const((4, 3 * DN_DIM)),
                  const((1, LANES)),
                  const((1, LANES)),
                  const((1, DN_HEAD_DIM))],
        out_specs=pl.BlockSpec((1, r, D_CONV + DN_DIM), lambda bi, t: (bi, t, 0)),
        out_shape=jax.ShapeDtypeStruct((b, lp, D_CONV + DN_DIM), BF16),
        scratch_shapes=[pltpu.VMEM((r + 8, D_CONV), F32),
                        pltpu.VMEM((DN_HEADS, DN_HEAD_DIM, DN_HEAD_DIM), F32)],
        compiler_params=_cparams(("arbitrary", "arbitrary")),
        name="gdn_mixer",
    )(p, p, p, gt, _decay_selectors(), _shift_matrix(r), conv_a_w.astype(F32), dn_conv_w.astype(F32),
      pad_lanes(a_log), pad_lanes(dt_bias), dn_norm_w.astype(F32).reshape(1, DN_HEAD_DIM))


LOG2E = 1.4426950408889634
SWA_HALF = SWA_BLOCK // 2
SWA_KEYS = N_META + SWA_BLOCK + SWA_HALF


def _qkv_proj_kernel(x_ref, nw_ref, w_ref, hw_ref, o_ref):
    xn = _rms(x_ref[...], nw_ref[...]).astype(BF16)
    n_norm = hw_ref.shape[1]
    lo = lax.broadcasted_iota(jnp.int32, (1, LANES), 1) < SWA_HEAD_DIM
    for j in range(w_ref.shape[1] // LANES):
        cs = slice(j * LANES, (j + 1) * LANES)
        y = _dot(xn, w_ref[:, cs])
        if j * LANES < n_norm:
            y2 = y * y
            ms_lo = jnp.sum(jnp.where(lo, y2, 0.0), -1, keepdims=True) * (1.0 / SWA_HEAD_DIM)
            ms_hi = jnp.sum(jnp.where(lo, 0.0, y2), -1, keepdims=True) * (1.0 / SWA_HEAD_DIM)
            y = y * jnp.where(lo, lax.rsqrt(ms_lo + EPS), lax.rsqrt(ms_hi + EPS)) * hw_ref[:, cs]
        o_ref[:, cs] = y.astype(o_ref.dtype)


def _qkv_proj(x, nw, w, hw, *, tm):
    m, k = x.shape
    n = w.shape[1]
    return pl.pallas_call(
        _qkv_proj_kernel,
        grid=(m // tm,),
        in_specs=[pl.BlockSpec((tm, k), lambda i: (i, 0)),
                  pl.BlockSpec((1, k), lambda i: (0, 0)),
                  pl.BlockSpec((k, n), lambda i: (0, 0)),
                  pl.BlockSpec((1, hw.shape[1]), lambda i: (0, 0))],
        out_specs=pl.BlockSpec((tm, n), lambda i: (i, 0)),
        out_shape=jax.ShapeDtypeStruct((m, n), BF16),
        compiler_params=_cparams(("parallel",)),
        name="qkv_proj",
    )(x, nw, w, hw)


def _swa_kernel(sink_ref, q_ref, km_ref, kp_ref, kc_ref, vm_ref, vp_ref, vc_ref, o_ref):
    i = pl.program_id(1)
    hb = SWA_HALF
    nk = SWA_KEYS
    lo = lax.broadcasted_iota(jnp.int32, (1, LANES), 1) < SWA_HEAD_DIM

    bias = []
    for hf in range(2):
        r = hf * hb + lax.broadcasted_iota(jnp.int32, (hb, nk), 0)
        cidx = lax.broadcasted_iota(jnp.int32, (hb, nk), 1)
        n_prev = SWA_BLOCK - hf * hb
        prev_j = cidx - N_META + hf * hb
        cur_j = cidx - N_META - n_prev
        meta_ok = (cidx < N_META) & ((i > 0) | (cidx <= r - PAD_ROWS))
        prev_ok = (cidx >= N_META) & (cidx < N_META + n_prev) & (prev_j > r) & (i >= 2)
        cur_ok = (cidx >= N_META + n_prev) & (cur_j <= r) & (i >= 1)
        bh = jnp.where(meta_ok | prev_ok | cur_ok, 0.0, NEG)
        bias.append(jnp.concatenate([bh] * SWA_GROUP, axis=0))

    def windows(m_ref, p_ref, c_ref, slab):
        cs = slice(slab * LANES, (slab + 1) * LANES)
        full = jnp.concatenate([m_ref[0, :, cs], p_ref[0, :, cs], c_ref[0, :, cs]], axis=0)
        swapped = pltpu.roll(full, SWA_HEAD_DIM, 1)
        return full, swapped

    def split(full, swapped, parity):
        zero = jnp.zeros_like(full)
        if parity == 0:
            return jnp.where(lo, full, zero), jnp.where(lo, zero, swapped)
        return jnp.where(lo, swapped, zero), jnp.where(lo, zero, full)

    def window(x, hf):
        if hf == 0:
            return x[0:nk]
        return jnp.concatenate([x[0:N_META], x[N_META + hb:N_META + 2 * SWA_BLOCK]], axis=0)

    kslabs = [windows(km_ref, kp_ref, kc_ref, s) for s in range(SWA_KV_HEADS // 2)]
    vslabs = [windows(vm_ref, vp_ref, vc_ref, s) for s in range(SWA_KV_HEADS // 2)]

    problems = [(kv, hf) for kv in range(SWA_KV_HEADS) for hf in range(2)]

    def logits(kv, hf):
        rows = slice(hf * hb, (hf + 1) * hb)
        base = kv * SWA_GROUP * SWA_HEAD_DIM
        lhs = jnp.concatenate([q_ref[0, rows, base:base + LANES],
                               q_ref[0, rows, base + LANES:base + 2 * LANES]], axis=0)
        k_lo, k_hi = split(*kslabs[kv // 2], kv % 2)
        return jnp.concatenate([_dot_nt(lhs, window(k_lo, hf)), _dot_nt(lhs, window(k_hi, hf))],
                               axis=0) + bias[hf]

    grp = lax.broadcasted_iota(jnp.int32, (SWA_GROUP * hb, 1), 0) // hb
    nxt = logits(*problems[0])
    for n, (kv, hf) in enumerate(problems):
        lg = nxt
        if n + 1 < len(problems):
            nxt = logits(*problems[n + 1])
        sink = jnp.zeros((SWA_GROUP * hb, 1), F32)
        for gi, g in enumerate((0, 2, 1, 3)):
            sink = jnp.where(grp == gi, sink_ref[kv * SWA_GROUP + g], sink)
        m = jnp.maximum(jnp.max(lg, axis=-1, keepdims=True), sink)
        e = jnp.exp2(lg - m)
        den = jnp.sum(e, axis=-1, keepdims=True) + jnp.exp2(sink - m)
        p = e.astype(BF16)
        v_lo, v_hi = split(*vslabs[kv // 2], kv % 2)
        o = _dot(p[0:2 * hb], window(v_lo, hf)) + _dot(p[2 * hb:4 * hb], window(v_hi, hf))
        rden = 1.0 / den
        o = o * jnp.where(lo, rden[0:2 * hb], rden[2 * hb:4 * hb])
        rows = slice(hf * hb, (hf + 1) * hb)
        base = kv * SWA_GROUP * SWA_HEAD_DIM
        o_ref[0, rows, base:base + LANES] = o[0:hb].astype(o_ref.dtype)
        o_ref[0, rows, base + LANES:base + 2 * LANES] = o[hb:2 * hb].astype(o_ref.dtype)


def _swa(qkv, sinks):
    b, lp, _ = qkv.shape
    blk = SWA_BLOCK
    nq = SWA_HEADS * SWA_HEAD_DIM
    nkv = SWA_KV_HEADS * SWA_HEAD_DIM
    kcol = nq // nkv
    vcol = kcol + 1
    meta_blk = PAD_ROWS // N_META
    prev = lambda t: jnp.maximum(t - 1, 0)
    return pl.pallas_call(
        _swa_kernel,
        grid=(b, lp // blk),
        in_specs=[pl.BlockSpec(memory_space=pltpu.SMEM),
                  pl.BlockSpec((1, blk, nq), lambda bi, t: (bi, t, 0)),
                  pl.BlockSpec((1, N_META, nkv), lambda bi, t: (bi, meta_blk, kcol)),
                  pl.BlockSpec((1, blk, nkv), lambda bi, t: (bi, prev(t), kcol)),
                  pl.BlockSpec((1, blk, nkv), lambda bi, t: (bi, t, kcol)),
                  pl.BlockSpec((1, N_META, nkv), lambda bi, t: (bi, meta_blk, vcol)),
                  pl.BlockSpec((1, blk, nkv), lambda bi, t: (bi, prev(t), vcol)),
                  pl.BlockSpec((1, blk, nkv), lambda bi, t: (bi, t, vcol))],
        out_specs=pl.BlockSpec((1, blk, nq), lambda bi, t: (bi, t, 0)),
        out_shape=jax.ShapeDtypeStruct((b, lp, nq), BF16),
        compiler_params=_cparams(("parallel", "arbitrary")),
        name="swa",
    )((sinks.astype(F32) * LOG2E), qkv, qkv, qkv, qkv, qkv, qkv, qkv)


def kernel(x, meta_tokens, attn_norm_w, ffn_norm_w, mix_w_in, conv_a_w, dn_conv_w, dn_a_log,
           dn_dt_bias, dn_norm_w, mix_w_out, swa_wq, swa_wk, swa_wv, swa_q_norm_w, swa_k_norm_w,
           swa_sinks, swa_wo, ffn_w_up, ffn_conv_w, ffn_w_down):
    b, seq, d = x.shape
    lp = ROW0 + seq
    meta = jnp.broadcast_to(meta_tokens[None].astype(x.dtype), (b, N_META, d))
    h = jnp.concatenate([jnp.zeros((b, PAD_ROWS, d), x.dtype), meta, x], axis=1)
    depth = attn_norm_w.shape[0]
    for layer in range(depth):
        i = layer // 2
        nw = attn_norm_w[layer].reshape(1, d)
        hf = h.reshape(b * lp, d)
        if layer % 2 == 0:
            w_in = mix_w_in[i]
            wg = jnp.zeros((d, LANES), BF16).at[:, :2 * DN_HEADS].set(w_in[:, P_MAIN:].astype(BF16))
            p, gt = _in_proj(hf, nw, w_in[:, :P_MAIN].astype(BF16), wg, tm=1024, tn=512)
            y = _mixer(p.reshape(b, lp, P_MAIN), gt.reshape(b, lp, LANES), conv_a_w[i], dn_conv_w[i],
                       dn_a_log[i], dn_dt_bias[i], dn_norm_w[i])
            hf = _matmul_res(y.reshape(b * lp, -1), mix_w_out[i].astype(BF16), hf, tm=1024)
        else:
            wqkv = jnp.concatenate([swa_wq[i], swa_wk[i], swa_wv[i]], axis=1).astype(BF16)
            head_w = jnp.concatenate([
                jnp.tile(swa_q_norm_w[i].astype(F32) * (SWA_HEAD_DIM ** -0.5 * LOG2E), SWA_HEADS),
                jnp.tile(swa_k_norm_w[i].astype(F32), SWA_KV_HEADS)]).reshape(1, -1)
            qkv = _qkv_proj(hf, nw, wqkv, head_w, tm=512)
            att = _swa(qkv.reshape(b, lp, -1), swa_sinks[i])
            hf = _matmul_res(att.reshape(b * lp, -1), swa_wo[i].astype(BF16), hf, tm=1024)
        h = _ffn(hf.reshape(b, lp, d), ffn_norm_w[layer], ffn_w_up[layer], ffn_conv_w[layer],
                 ffn_w_down[layer], tm=528)
    return h[:, ROW0:]
```

```python
import jax
import jax.numpy as jnp
from jax import lax
from jax.experimental import pallas as pl
from jax.experimental.pallas import tpu as pltpu

F32 = jnp.float32
BF16 = jnp.bfloat16

N_META = 16
D_CONV = 512
DN_HEADS = 4
DN_HEAD_DIM = 128
DN_DIM = DN_HEADS * DN_HEAD_DIM
DN_CHUNK = 64
SWA_HEADS = 16
SWA_KV_HEADS = 4
SWA_GROUP = SWA_HEADS // SWA_KV_HEADS
SWA_HEAD_DIM = 64
SWA_BLOCK = 128
EPS = 1e-6

LANES = 128
ROW0 = 128
PAD_ROWS = ROW0 - N_META
P_MAIN = 3 * D_CONV + 4 * DN_DIM
NEG = -1e30
VMEM_LIMIT = 56 * 1024 * 1024


def _cparams(sem):
    return pltpu.CompilerParams(dimension_semantics=sem, vmem_limit_bytes=VMEM_LIMIT)


def _rms(x, w):
    ms = jnp.mean(x * x, axis=-1, keepdims=True)
    return x * lax.rsqrt(ms + EPS) * w


def _sigmoid(x):
    return 1.0 / (1.0 + jnp.exp(-x))


def _silu(x):
    return x * _sigmoid(x)


def _dot(a, b):
    return jnp.dot(a, b, preferred_element_type=F32)


def _dot_nt(a, b):
    return lax.dot_general(a, b, (((1,), (1,)), ((), ())), preferred_element_type=F32)


def _blockdiag(a, b):
    za = jnp.zeros_like(a)
    return jnp.concatenate([jnp.concatenate([a, za], axis=1), jnp.concatenate([za, b], axis=1)], axis=0)


MXU_COLS = 256


def _resident(shape):
    return pl.BlockSpec(shape, lambda *_: (0,) * len(shape), pipeline_mode=pl.Buffered(1))


def _in_proj_kernel(x_ref, nw_ref, w_ref, wg_ref, o_ref, og_ref):
    xn = _rms(x_ref[...], nw_ref[...]).astype(BF16)
    for j in range(w_ref.shape[1] // MXU_COLS):
        cs = slice(j * MXU_COLS, (j + 1) * MXU_COLS)
        o_ref[:, cs] = _dot(xn, w_ref[:, cs]).astype(o_ref.dtype)
    og_ref[...] = _dot(xn, wg_ref[...])


def _in_proj(x, nw, w, wg, *, tm):
    m, k = x.shape
    n = w.shape[1]
    return pl.pallas_call(
        _in_proj_kernel,
        grid=(m // tm,),
        in_specs=[pl.BlockSpec((tm, k), lambda i: (i, 0)),
                  _resident((1, k)),
                  _resident((k, n)),
                  _resident((k, LANES))],
        out_specs=[pl.BlockSpec((tm, n), lambda i: (i, 0)),
                   pl.BlockSpec((tm, LANES), lambda i: (i, 0))],
        out_shape=[jax.ShapeDtypeStruct((m, n), BF16),
                   jax.ShapeDtypeStruct((m, LANES), F32)],
        compiler_params=_cparams(("parallel",)),
        name="in_proj",
    )(x, nw, w, wg)


FFN_HALO = 16
FFN_CHUNK = 256


def _ffn_kernel(x_ref, xp_ref, nw_ref, wg_ref, wv_ref, cw_ref, wd_ref, o_ref, act_ref):
    t = pl.program_id(1)
    tm = x_ref.shape[1]
    x = x_ref[0]
    xn = _rms(x, nw_ref[...]).astype(BF16)
    xp = jnp.where(t > 0, xp_ref[0], 0.0)
    xn_ext = jnp.concatenate([_rms(xp, nw_ref[...]).astype(BF16), xn], axis=0)
    for j in range(act_ref.shape[1] // FFN_CHUNK):
        cs = slice(j * FFN_CHUNK, (j + 1) * FFN_CHUNK)
        g = _dot(xn_ext, wg_ref[:, cs])
        v = _dot(xn, wv_ref[:, cs])
        g1 = pltpu.roll(g, 1, 0)[FFN_HALO:]
        g2 = pltpu.roll(g, 2, 0)[FFN_HALO:]
        conv = cw_ref[0:1, cs] * g2 + cw_ref[1:2, cs] * g1 + cw_ref[2:3, cs] * g[FFN_HALO:]
        act_ref[:, cs] = (_silu(conv) * v).astype(BF16)
    o_ref[0] = x + _dot(act_ref[...], wd_ref[...])


def _ffn(h, nw, w_up, conv_w, w_down, *, tm):
    b, lp, d = h.shape
    f = w_down.shape[0]
    halo_blk = lambda t: jnp.maximum(t * (tm // FFN_HALO) - 1, 0)
    return pl.pallas_call(
        _ffn_kernel,
        grid=(b, lp // tm),
        in_specs=[pl.BlockSpec((1, tm, d), lambda bi, t: (bi, t, 0)),
                  pl.BlockSpec((1, FFN_HALO, d), lambda bi, t: (bi, halo_blk(t), 0)),
                  _resident((1, d)),
                  _resident((d, f)),
                  _resident((d, f)),
                  _resident((3, f)),
                  _resident((f, d))],
        out_specs=pl.BlockSpec((1, tm, d), lambda bi, t: (bi, t, 0)),
        out_shape=jax.ShapeDtypeStruct((b, lp, d), F32),
        scratch_shapes=[pltpu.VMEM((tm, f), BF16)],
        compiler_params=_cparams(("parallel", "arbitrary")),
        name="ffn",
    )(h, h, nw.reshape(1, d), w_up[:, :f].astype(BF16), w_up[:, f:].astype(BF16),
      conv_w.astype(F32), w_down.astype(BF16))


MIX_PAIR = 2 * DN_CHUNK
MIX_PAIRS = 3
CONV_CARRY = 16
MIX_ROWS = MIX_PAIR * MIX_PAIRS


def _mixer_kernel(p_ref, pa_ref, pq_ref, gt_ref, h_ref, sel_ref, conva_ref, dnconv_ref, alog_ref,
                  dtb_ref, dnw_ref, wo_ref, o_ref, y_ref, s_ref):
    r = MIX_ROWS
    pr = MIX_PAIR
    c = DN_CHUNK
    dh = DN_HEAD_DIM
    t = pl.program_id(1)
    has_prev = t > 0

    @pl.when(t == 0)
    def _():
        s_ref[...] = jnp.zeros(s_ref.shape, F32)

    prev_a = (pa_ref[0, :, 0:D_CONV].astype(F32) * pa_ref[0, :, 2 * D_CONV:3 * D_CONV].astype(F32))
    xa = jnp.concatenate([jnp.where(has_prev, prev_a, 0.0),
                          p_ref[0, :, 0:D_CONV].astype(F32) * p_ref[0, :, 2 * D_CONV:3 * D_CONV].astype(F32)],
                         axis=0)
    ca = (conva_ref[0:1, :] * pltpu.roll(xa, 2, 0)[CONV_CARRY:]
          + conva_ref[1:2, :] * pltpu.roll(xa, 1, 0)[CONV_CARRY:]
          + conva_ref[2:3, :] * xa[CONV_CARRY:])
    y_ref[:, 0:D_CONV] = (p_ref[0, :, D_CONV:2 * D_CONV].astype(F32) * ca).astype(BF16)

    q0 = 3 * D_CONV
    prev_q = jnp.where(has_prev, pq_ref[0].astype(F32), 0.0)
    xe = jnp.concatenate([prev_q, p_ref[0, :, q0:q0 + 3 * DN_DIM].astype(F32)], axis=0)
    cq = dnconv_ref[3:4, :] * xe[CONV_CARRY:]
    for j in range(3):
        cq = cq + dnconv_ref[j:j + 1, :] * pltpu.roll(xe, 3 - j, 0)[CONV_CARRY:]
    qkv = _silu(cq)

    gt = gt_ref[0]
    row = t * r + lax.broadcasted_iota(jnp.int32, (r, LANES), 0)
    live = row >= PAD_ROWS
    beta_all = jnp.where(live, _sigmoid(gt), 0.0)
    xg = gt + dtb_ref[...]
    softplus = jnp.maximum(xg, 0.0) + jnp.log1p(jnp.exp(-jnp.abs(xg)))
    g_all = jnp.where(live, -jnp.exp(alog_ref[...]) * softplus, 0.0)
    g_hi = g_all.astype(BF16)
    g_r1 = g_all - g_hi.astype(F32)
    g_mid = g_r1.astype(BF16)
    g_lo = (g_r1 - g_mid.astype(F32)).astype(BF16)
    sel = sel_ref[...]

    ii = lax.broadcasted_iota(jnp.int32, (pr, pr), 0)
    jj = lax.broadcasted_iota(jnp.int32, (pr, pr), 1)
    same = (ii >= c) == (jj >= c)
    incl = same & (ii >= jj)
    strict = same & (ii > jj)
    first = lax.broadcasted_iota(jnp.int32, (pr, 1), 0) < c

    keys = [(pi, h) for pi in range(MIX_PAIRS) for h in range(DN_HEADS)]
    local = {}
    amat = {}
    for pi in range(MIX_PAIRS):
        ps = slice(pi * pr, (pi + 1) * pr)
        cums = _dot(sel, g_hi[ps]) + (_dot(sel, g_mid[ps]) + _dot(sel, g_lo[ps]))
        cdec = cums[0:pr]
        dlast = tuple(jnp.broadcast_to(cums[pr + 8 * ci:pr + 8 * ci + 1], (pr, LANES)) for ci in range(2))
        cdec_t = cdec.T
        for h in range(DN_HEADS):
            q = qkv[ps, h * dh:(h + 1) * dh]
            k = qkv[ps, DN_DIM + h * dh:DN_DIM + (h + 1) * dh]
            v = qkv[ps, 2 * DN_DIM + h * dh:2 * DN_DIM + (h + 1) * dh]
            q = q * lax.rsqrt(jnp.sum(q * q, -1, keepdims=True) + EPS) * (dh ** -0.5)
            k = k * lax.rsqrt(jnp.sum(k * k, -1, keepdims=True) + EPS)
            gl = DN_HEADS + h
            beta = beta_all[ps, h:h + 1]
            cd_col = cdec[:, gl:gl + 1]
            cd_row = cdec_t[gl:gl + 1, :]
            dmask = jnp.where(incl, jnp.exp(jnp.where(incl, cd_col - cd_row, 0.0)), 0.0)
            kb = k.astype(BF16)
            qk_kk = _dot_nt(jnp.concatenate([q.astype(BF16), kb], axis=0), kb)
            amat[pi, h] = jnp.where(strict, beta * qk_kk[pr:2 * pr] * dmask, 0.0)
            e_cd = jnp.exp(cd_col)
            dl_own = jnp.where(first, dlast[0][:, gl:gl + 1], dlast[1][:, gl:gl + 1])
            local[pi, h] = dict(
                rhs=jnp.concatenate([v * beta, k * (beta * e_cd)], axis=1),
                qk=(qk_kk[0:pr] * dmask).astype(BF16),
                q_dec=(q * e_cd).astype(BF16),
                k_dec_t=(k * jnp.exp(dl_own - cd_col)).T.astype(BF16),
                g_last=[jnp.exp(d[:, gl:gl + 1]) for d in dlast])

    pairs = [(pi, hp) for pi in range(MIX_PAIRS) for hp in range(DN_HEADS // 2)]
    tm = {key: -amat[key] for key in keys}
    pw = dict(amat)
    for _ in range(5):
        for pi, hp in pairs:
            ka, kb_ = (pi, 2 * hp), (pi, 2 * hp + 1)
            pa_, pb_ = pw[ka].astype(BF16), pw[kb_].astype(BF16)
            sq = _dot(jnp.concatenate([pa_, pb_], axis=1), _blockdiag(pa_, pb_))
            pw[ka], pw[kb_] = sq[:, 0:pr], sq[:, pr:2 * pr]
        for pi, hp in pairs:
            ka, kb_ = (pi, 2 * hp), (pi, 2 * hp + 1)
            prod = _dot(jnp.concatenate([tm[ka].astype(BF16), tm[kb_].astype(BF16)], axis=1),
                        _blockdiag(pw[ka].astype(BF16), pw[kb_].astype(BF16)))
            tm[ka] = tm[ka] + pw[ka] + prod[:, 0:pr]
            tm[kb_] = tm[kb_] + pw[kb_] + prod[:, pr:2 * pr]
    for key in keys:
        rhs = local[key]["rhs"]
        sol = rhs + _dot(tm[key].astype(BF16), rhs.astype(BF16))
        local[key]["u"] = sol[:, 0:dh]
        local[key]["w"] = sol[:, dh:2 * dh].astype(BF16)

    z0 = 3 * D_CONV + 3 * DN_DIM
    state = [s_ref[h] for h in range(DN_HEADS)]
    outs = [[] for _ in range(DN_HEADS)]
    zeros = jnp.zeros((c, dh), BF16)
    hpairs = range(DN_HEADS // 2)
    for pi in range(MIX_PAIRS):
        for ci in range(2):
            rs = slice(ci * c, (ci + 1) * c)
            ws_qs = []
            for hp in hpairs:
                lhs = jnp.concatenate(
                    [jnp.concatenate([local[pi, h]["w"][rs], local[pi, h]["q_dec"][rs]], axis=0)
                     for h in (2 * hp, 2 * hp + 1)], axis=1)
                ws_qs.append(_dot(lhs, _blockdiag(state[2 * hp].astype(BF16), state[2 * hp + 1].astype(BF16))))
            v_full = []
            for h in range(DN_HEADS):
                ws = ws_qs[h // 2][0:c, (h % 2) * dh:(h % 2 + 1) * dh]
                vb = (local[pi, h]["u"][rs] - ws).astype(BF16)
                v_full.append(jnp.concatenate([vb, zeros] if ci == 0 else [zeros, vb], axis=0))
            for hp in hpairs:
                ha, hb_ = 2 * hp, 2 * hp + 1
                upd = _dot(jnp.concatenate([local[pi, ha]["k_dec_t"], local[pi, hb_]["k_dec_t"]], axis=1),
                           _blockdiag(v_full[ha], v_full[hb_]))
                state[ha] = state[ha] * local[pi, ha]["g_last"][ci] + upd[:, 0:dh]
                state[hb_] = state[hb_] * local[pi, hb_]["g_last"][ci] + upd[:, dh:2 * dh]
            for hp in hpairs:
                ha, hb_ = 2 * hp, 2 * hp + 1
                intra = _dot(jnp.concatenate([local[pi, ha]["qk"][rs], local[pi, hb_]["qk"][rs]], axis=1),
                             _blockdiag(v_full[ha], v_full[hb_]))
                outs[ha].append(ws_qs[hp][c:2 * c, 0:dh] + intra[:, 0:dh])
                outs[hb_].append(ws_qs[hp][c:2 * c, dh:2 * dh] + intra[:, dh:2 * dh])
    for h in range(DN_HEADS):
        s_ref[h] = state[h]
        o = jnp.concatenate(outs[h], axis=0)
        z = p_ref[0, :, z0 + h * dh:z0 + (h + 1) * dh].astype(F32)
        y = _rms(o, dnw_ref[...]) * _silu(z)
        y_ref[:, D_CONV + h * dh:D_CONV + (h + 1) * dh] = y.astype(BF16)

    o_ref[0] = h_ref[0] + _dot(y_ref[...], wo_ref[...])


def _decay_selectors():
    r, c = MIX_PAIR, DN_CHUNK
    i = jnp.arange(r)[:, None]
    j = jnp.arange(r)[None, :]
    tri = ((i // c) == (j // c)) & (i >= j)
    last0 = jnp.broadcast_to(j < c, (8, r))
    last1 = jnp.broadcast_to(j >= c, (8, r))
    return jnp.concatenate([tri, last0, last1], axis=0).astype(BF16)


def _mixer(p, gt, h, conv_a_w, dn_conv_w, a_log, dt_bias, dn_norm_w, w_out):
    b, lp, _ = p.shape
    d = h.shape[-1]
    r = MIX_ROWS
    pad_lanes = lambda vec: jnp.zeros((1, LANES), F32).at[0, DN_HEADS:2 * DN_HEADS].set(vec.astype(F32))
    const = lambda shape: pl.BlockSpec(shape, lambda bi, t: (0,) * len(shape))
    prev_rows = lambda t: jnp.maximum(t * (r // CONV_CARRY) - 1, 0)
    qkv_cols = 3 * DN_DIM
    return pl.pallas_call(
        _mixer_kernel,
        grid=(b, lp // r),
        in_specs=[pl.BlockSpec((1, r, P_MAIN), lambda bi, t: (bi, t, 0)),
                  pl.BlockSpec((1, CONV_CARRY, qkv_cols), lambda bi, t: (bi, prev_rows(t), 0)),
                  pl.BlockSpec((1, CONV_CARRY, qkv_cols), lambda bi, t: (bi, prev_rows(t), 1)),
                  pl.BlockSpec((1, r, LANES), lambda bi, t: (bi, t, 0)),
                  pl.BlockSpec((1, r, d), lambda bi, t: (bi, t, 0)),
                  const((MIX_PAIR + 16, MIX_PAIR)),
                  const((3, D_CONV)),
                  const((4, 3 * DN_DIM)),
                  const((1, LANES)),
                  const((1, LANES)),
                  const((1, DN_HEAD_DIM)),
                  _resident((D_CONV + DN_DIM, d))],
        out_specs=pl.BlockSpec((1, r, d), lambda bi, t: (bi, t, 0)),
        out_shape=jax.ShapeDtypeStruct((b, lp, d), F32),
        scratch_shapes=[pltpu.VMEM((r, D_CONV + DN_DIM), BF16),
                        pltpu.VMEM((DN_HEADS, DN_HEAD_DIM, DN_HEAD_DIM), F32)],
        compiler_params=_cparams(("arbitrary", "arbitrary")),
        name="gdn_mixer",
    )(p, p, p, gt, h, _decay_selectors(), conv_a_w.astype(F32), dn_conv_w.astype(F32),
      pad_lanes(a_log), pad_lanes(dt_bias), dn_norm_w.astype(F32).reshape(1, DN_HEAD_DIM),
      w_out.astype(BF16))


LOG2E = 1.4426950408889634
SWA_HALF = SWA_BLOCK // 2
SWA_KEYS = N_META + SWA_BLOCK + SWA_HALF


def _qkv_proj_kernel(x_ref, nw_ref, w_ref, hw_ref, o_ref):
    xn = _rms(x_ref[...], nw_ref[...]).astype(BF16)
    n_norm = hw_ref.shape[1]
    lo = lax.broadcasted_iota(jnp.int32, (1, LANES), 1) < SWA_HEAD_DIM
    for j in range(w_ref.shape[1] // MXU_COLS):
        yy = _dot(xn, w_ref[:, j * MXU_COLS:(j + 1) * MXU_COLS])
        for s in range(MXU_COLS // LANES):
            cs = slice(j * MXU_COLS + s * LANES, j * MXU_COLS + (s + 1) * LANES)
            y = yy[:, s * LANES:(s + 1) * LANES]
            if cs.start < n_norm:
                y2 = y * y
                ms_lo = jnp.sum(jnp.where(lo, y2, 0.0), -1, keepdims=True) * (1.0 / SWA_HEAD_DIM)
                ms_hi = jnp.sum(jnp.where(lo, 0.0, y2), -1, keepdims=True) * (1.0 / SWA_HEAD_DIM)
                y = y * jnp.where(lo, lax.rsqrt(ms_lo + EPS), lax.rsqrt(ms_hi + EPS)) * hw_ref[:, cs]
            o_ref[:, cs] = y.astype(o_ref.dtype)


def _qkv_proj(x, nw, w, hw, *, tm):
    m, k = x.shape
    n = w.shape[1]
    return pl.pallas_call(
        _qkv_proj_kernel,
        grid=(m // tm,),
        in_specs=[pl.BlockSpec((tm, k), lambda i: (i, 0)),
                  _resident((1, k)),
                  _resident((k, n)),
                  _resident((1, hw.shape[1]))],
        out_specs=pl.BlockSpec((tm, n), lambda i: (i, 0)),
        out_shape=jax.ShapeDtypeStruct((m, n), BF16),
        compiler_params=_cparams(("parallel",)),
        name="qkv_proj",
    )(x, nw, w, hw)


SWA_STEP_BLOCKS = 3


def _swa_kernel(sink_ref, q_ref, km_ref, kp_ref, kc_ref, vm_ref, vp_ref, vc_ref, h_ref, wo_ref, o_ref,
                att_ref):
    t = pl.program_id(1)
    blk = SWA_BLOCK
    hb = SWA_HALF
    nk = SWA_KEYS
    lo = lax.broadcasted_iota(jnp.int32, (1, LANES), 1) < SWA_HEAD_DIM

    def mask_bias(i, hf):
        r = hf * hb + lax.broadcasted_iota(jnp.int32, (hb, nk), 0)
        cidx = lax.broadcasted_iota(jnp.int32, (hb, nk), 1)
        n_prev = blk - hf * hb
        prev_j = cidx - N_META + hf * hb
        cur_j = cidx - N_META - n_prev
        meta_ok = (cidx < N_META) & ((i > 0) | (cidx <= r - PAD_ROWS))
        prev_ok = (cidx >= N_META) & (cidx < N_META + n_prev) & (prev_j > r) & (i >= 2)
        cur_ok = (cidx >= N_META + n_prev) & (cur_j <= r) & (i >= 1)
        bh = jnp.where(meta_ok | prev_ok | cur_ok, 0.0, NEG)
        return jnp.concatenate([bh] * SWA_GROUP, axis=0)

    def slab_rows(m_ref, p_ref, c_ref, slab):
        cs = slice(slab * LANES, (slab + 1) * LANES)
        full = jnp.concatenate([m_ref[0, :, cs], p_ref[0, :, cs], c_ref[0, :, cs]], axis=0)
        swapped = pltpu.roll(full, SWA_HEAD_DIM, 1)
        zero = jnp.zeros_like(full)
        even = (jnp.where(lo, full, zero), jnp.where(lo, zero, swapped))
        odd = (jnp.where(lo, swapped, zero), jnp.where(lo, zero, full))
        return even, odd

    def window(x, j, hf):
        a = N_META + j * blk + hf * hb
        return jnp.concatenate([x[0:N_META], x[a:a + nk - N_META]], axis=0)

    kslabs = [slab_rows(km_ref, kp_ref, kc_ref, s) for s in range(SWA_KV_HEADS // 2)]
    vslabs = [slab_rows(vm_ref, vp_ref, vc_ref, s) for s in range(SWA_KV_HEADS // 2)]
    bias = {(j, hf): mask_bias(t * SWA_STEP_BLOCKS + j, hf)
            for j in range(SWA_STEP_BLOCKS) for hf in range(2)}
    problems = [(j, kv, hf) for j in range(SWA_STEP_BLOCKS) for kv in range(SWA_KV_HEADS) for hf in range(2)]

    def logits(j, kv, hf):
        rows = slice(j * blk + hf * hb, j * blk + (hf + 1) * hb)
        base = kv * SWA_GROUP * SWA_HEAD_DIM
        lhs = jnp.concatenate([q_ref[0, rows, base:base + LANES],
                               q_ref[0, rows, base + LANES:base + 2 * LANES]], axis=0)
        k_lo, k_hi = kslabs[kv // 2][kv % 2]
        return jnp.concatenate([_dot_nt(lhs, window(k_lo, j, hf)), _dot_nt(lhs, window(k_hi, j, hf))],
                               axis=0) + bias[j, hf]

    grp = lax.broadcasted_iota(jnp.int32, (SWA_GROUP * hb, 1), 0) // hb
    sinks = []
    for kv in range(SWA_KV_HEADS):
        sink = jnp.zeros((SWA_GROUP * hb, 1), F32)
        for gi, g in enumerate((0, 2, 1, 3)):
            sink = jnp.where(grp == gi, sink_ref[kv * SWA_GROUP + g], sink)
        sinks.append(sink)

    def row_max(st, j, kv, hf):
        st["m"] = jnp.maximum(jnp.max(st["lg"], axis=-1, keepdims=True), sinks[kv])

    def exp_sum(st, j, kv, hf):
        e = jnp.exp2(st.pop("lg") - st["m"])
        st["den"] = jnp.sum(e, axis=-1, keepdims=True) + jnp.exp2(sinks[kv] - st.pop("m"))
        st["p"] = e.astype(BF16)

    def weighted_sum(st, j, kv, hf):
        p = st.pop("p")
        v_lo, v_hi = vslabs[kv // 2][kv % 2]
        st["o"] = (_dot(p[0:2 * hb], window(v_lo, j, hf))
                   + _dot(p[2 * hb:4 * hb], window(v_hi, j, hf)))
        st["rden"] = 1.0 / st.pop("den")

    def normalise(st, j, kv, hf):
        rden = st.pop("rden")
        o = st.pop("o") * jnp.where(lo, rden[0:2 * hb], rden[2 * hb:4 * hb])
        rows = slice(j * blk + hf * hb, j * blk + (hf + 1) * hb)
        base = kv * SWA_GROUP * SWA_HEAD_DIM
        att_ref[rows, base:base + LANES] = o[0:hb].astype(BF16)
        att_ref[rows, base + LANES:base + 2 * LANES] = o[hb:2 * hb].astype(BF16)

    stages = (None, row_max, exp_sum, weighted_sum, normalise)
    states = [dict() for _ in problems]
    for step in range(len(problems) + len(stages) - 1):
        for depth, stage in enumerate(stages):
            n = step - depth
            if 0 <= n < len(problems):
                if stage is None:
                    states[n]["lg"] = logits(*problems[n])
                else:
                    stage(states[n], *problems[n])

    o_ref[0] = h_ref[0] + _dot(att_ref[...], wo_ref[...])


def _swa(qkv, sinks, h, wo):
    b, lp, _ = qkv.shape
    d = h.shape[-1]
    blk = SWA_BLOCK
    rows = SWA_STEP_BLOCKS * blk
    nq = SWA_HEADS * SWA_HEAD_DIM
    nkv = SWA_KV_HEADS * SWA_HEAD_DIM
    kcol = nq // nkv
    vcol = kcol + 1
    meta_blk = PAD_ROWS // N_META
    prev = lambda t: jnp.maximum(t * SWA_STEP_BLOCKS - 1, 0)
    return pl.pallas_call(
        _swa_kernel,
        grid=(b, lp // rows),
        in_specs=[pl.BlockSpec(memory_space=pltpu.SMEM),
                  pl.BlockSpec((1, rows, nq), lambda bi, t: (bi, t, 0)),
                  pl.BlockSpec((1, N_META, nkv), lambda bi, t: (bi, meta_blk, kcol)),
                  pl.BlockSpec((1, blk, nkv), lambda bi, t: (bi, prev(t), kcol)),
                  pl.BlockSpec((1, rows, nkv), lambda bi, t: (bi, t, kcol)),
                  pl.BlockSpec((1, N_META, nkv), lambda bi, t: (bi, meta_blk, vcol)),
                  pl.BlockSpec((1, blk, nkv), lambda bi, t: (bi, prev(t), vcol)),
                  pl.BlockSpec((1, rows, nkv), lambda bi, t: (bi, t, vcol)),
                  pl.BlockSpec((1, rows, d), lambda bi, t: (bi, t, 0)),
                  _resident((nq, d))],
        out_specs=pl.BlockSpec((1, rows, d), lambda bi, t: (bi, t, 0)),
        out_shape=jax.ShapeDtypeStruct((b, lp, d), F32),
        scratch_shapes=[pltpu.VMEM((rows, nq), BF16)],
        compiler_params=_cparams(("parallel", "arbitrary")),
        name="swa",
    )((sinks.astype(F32) * LOG2E), qkv, qkv, qkv, qkv, qkv, qkv, qkv, h, wo.astype(BF16))


def kernel(x, meta_tokens, attn_norm_w, ffn_norm_w, mix_w_in, conv_a_w, dn_conv_w, dn_a_log,
           dn_dt_bias, dn_norm_w, mix_w_out, swa_wq, swa_wk, swa_wv, swa_q_norm_w, swa_k_norm_w,
           swa_sinks, swa_wo, ffn_w_up, ffn_conv_w, ffn_w_down):
    b, seq, d = x.shape
    lp = ROW0 + seq
    meta = jnp.broadcast_to(meta_tokens[None].astype(x.dtype), (b, N_META, d))
    h = jnp.concatenate([jnp.zeros((b, PAD_ROWS, d), x.dtype), meta, x], axis=1)
    depth = attn_norm_w.shape[0]
    for layer in range(depth):
        i = layer // 2
        nw = attn_norm_w[layer].reshape(1, d)
        hf = h.reshape(b * lp, d)
        if layer % 2 == 0:
            w_in = mix_w_in[i]
            wg = jnp.zeros((d, LANES), BF16).at[:, :2 * DN_HEADS].set(w_in[:, P_MAIN:].astype(BF16))
            p, gt = _in_proj(hf, nw, w_in[:, :P_MAIN].astype(BF16), wg, tm=1024)
            h = _mixer(p.reshape(b, lp, P_MAIN), gt.reshape(b, lp, LANES), h, conv_a_w[i], dn_conv_w[i],
                       dn_a_log[i], dn_dt_bias[i], dn_norm_w[i], mix_w_out[i])
        else:
            wqkv = jnp.concatenate([swa_wq[i], swa_wk[i], swa_wv[i]], axis=1).astype(BF16)
            head_w = jnp.concatenate([
                jnp.tile(swa_q_norm_w[i].astype(F32) * (SWA_HEAD_DIM ** -0.5 * LOG2E), SWA_HEADS),
                jnp.tile(swa_k_norm_w[i].astype(F32), SWA_KV_HEADS)]).reshape(1, -1)
            qkv = _qkv_proj(hf, nw, wqkv, head_w, tm=512)
            h = _swa(qkv.reshape(b, lp, -1), swa_sinks[i], h, swa_wo[i])
        h = _ffn(h, ffn_norm_w[layer], ffn_w_up[layer], ffn_conv_w[layer],
                 ffn_w_down[layer], tm=528)
    return h[:, ROW0:]
```

```python
import functools

import jax
import jax.numpy as jnp
from jax import lax
from jax.experimental import pallas as pl
from jax.experimental.pallas import tpu as pltpu

F32 = jnp.float32
BF16 = jnp.bfloat16

N_META = 16
D_CONV = 512
DN_HEADS = 4
DN_HEAD_DIM = 128
DN_DIM = DN_HEADS * DN_HEAD_DIM
DN_CHUNK = 64
SWA_HEADS = 16
SWA_KV_HEADS = 4
SWA_GROUP = SWA_HEADS // SWA_KV_HEADS
SWA_HEAD_DIM = 64
SWA_BLOCK = 128
EPS = 1e-6

LANES = 128
ROW0 = 128
PAD_ROWS = ROW0 - N_META
P_MAIN = 3 * D_CONV + 4 * DN_DIM
NEG = -1e30
VMEM_LIMIT = 56 * 1024 * 1024


def _cparams(sem):
    return pltpu.CompilerParams(dimension_semantics=sem, vmem_limit_bytes=VMEM_LIMIT)


def _rms(x, w):
    ms = jnp.mean(x * x, axis=-1, keepdims=True)
    return x * lax.rsqrt(ms + EPS) * w


def _sigmoid(x):
    return 1.0 / (1.0 + jnp.exp(-x))


def _silu(x):
    return x * _sigmoid(x)


def _dot(a, b):
    return jnp.dot(a, b, preferred_element_type=F32)


def _dot_nt(a, b):
    return lax.dot_general(a, b, (((1,), (1,)), ((), ())), preferred_element_type=F32)


def _blockdiag(a, b):
    za = jnp.zeros_like(a)
    return jnp.concatenate([jnp.concatenate([a, za], axis=1), jnp.concatenate([za, b], axis=1)], axis=0)


MXU_COLS = 256


def _resident(shape):
    return pl.BlockSpec(shape, lambda *_: (0,) * len(shape), pipeline_mode=pl.Buffered(1))


def _in_proj_kernel(x_ref, nw_ref, w_ref, wg_ref, o_ref, og_ref):
    xn = _rms(x_ref[...], nw_ref[...]).astype(BF16)
    for j in range(w_ref.shape[1] // MXU_COLS):
        cs = slice(j * MXU_COLS, (j + 1) * MXU_COLS)
        o_ref[:, cs] = _dot(xn, w_ref[:, cs]).astype(o_ref.dtype)
    og_ref[...] = _dot(xn, wg_ref[...])


def _in_proj(x, nw, w, wg, *, tm):
    m, k = x.shape
    n = w.shape[1]
    return pl.pallas_call(
        _in_proj_kernel,
        grid=(m // tm,),
        in_specs=[pl.BlockSpec((tm, k), lambda i: (i, 0)),
                  _resident((1, k)),
                  _resident((k, n)),
                  _resident((k, LANES))],
        out_specs=[pl.BlockSpec((tm, n), lambda i: (i, 0)),
                   pl.BlockSpec((tm, LANES), lambda i: (i, 0))],
        out_shape=[jax.ShapeDtypeStruct((m, n), BF16),
                   jax.ShapeDtypeStruct((m, LANES), F32)],
        compiler_params=_cparams(("parallel",)),
        name="in_proj",
    )(x, nw, w, wg)


FFN_HALO = 16
FFN_CHUNK = 256


def _ffn_kernel(x_ref, xp_ref, nw_ref, wg_ref, wv_ref, cw_ref, wd_ref, o_ref, act_ref, *, seq_start):
    x = x_ref[...]
    xn = _rms(x, nw_ref[...]).astype(BF16)
    xp = xp_ref[...]
    if seq_start:
        xp = jnp.where(pl.program_id(1) > 0, xp, 0.0)
    xn_ext = jnp.concatenate([_rms(xp, nw_ref[...]).astype(BF16), xn], axis=0)
    for j in range(act_ref.shape[1] // FFN_CHUNK):
        cs = slice(j * FFN_CHUNK, (j + 1) * FFN_CHUNK)
        g = _dot(xn_ext, wg_ref[:, cs])
        v = _dot(xn, wv_ref[:, cs])
        g1 = pltpu.roll(g, 1, 0)[FFN_HALO:]
        g2 = pltpu.roll(g, 2, 0)[FFN_HALO:]
        conv = cw_ref[0:1, cs] * g2 + cw_ref[1:2, cs] * g1 + cw_ref[2:3, cs] * g[FFN_HALO:]
        act_ref[:, cs] = (_silu(conv) * v).astype(BF16)
    o_ref[...] = x + _dot(act_ref[...], wd_ref[...])


def _ffn(h, nw, w_up, conv_w, w_down, *, tm, first_row=0):
    b, lp, d = h.shape
    f = w_down.shape[0]
    rows = lp - first_row
    nt = rows // tm
    h2 = h.reshape(b * lp, d)
    if first_row == 0:
        x_spec = pl.BlockSpec((tm, d), lambda bi, t: (bi * nt + t, 0))
        halo_spec = pl.BlockSpec(
            (FFN_HALO, d), lambda bi, t: (jnp.maximum((bi * lp + t * tm) // FFN_HALO - 1, 0), 0))
    else:
        row0 = lambda bi, t: bi * lp + first_row + t * tm
        x_spec = pl.BlockSpec((pl.Element(tm), pl.Element(d)),
                              lambda bi, t: (pl.multiple_of(row0(bi, t), FFN_HALO), 0))
        halo_spec = pl.BlockSpec((pl.Element(FFN_HALO), pl.Element(d)),
                                 lambda bi, t: (pl.multiple_of(row0(bi, t) - FFN_HALO, FFN_HALO), 0))
    out = pl.pallas_call(
        functools.partial(_ffn_kernel, seq_start=first_row == 0),
        grid=(b, nt),
        in_specs=[x_spec,
                  halo_spec,
                  _resident((1, d)),
                  _resident((d, f)),
                  _resident((d, f)),
                  _resident((3, f)),
                  _resident((f, d))],
        out_specs=pl.BlockSpec((tm, d), lambda bi, t: (bi * nt + t, 0)),
        out_shape=jax.ShapeDtypeStruct((b * rows, d), F32),
        scratch_shapes=[pltpu.VMEM((tm, f), BF16)],
        compiler_params=_cparams(("parallel", "arbitrary")),
        name="ffn",
    )(h2, h2, nw.reshape(1, d), w_up[:, :f].astype(BF16), w_up[:, f:].astype(BF16),
      conv_w.astype(F32), w_down.astype(BF16))
    return out.reshape(b, rows, d)


MIX_PAIR = 2 * DN_CHUNK
MIX_PAIRS = 3
CONV_CARRY = 16
MIX_ROWS = MIX_PAIR * MIX_PAIRS
MIX_BATCH = 2


def _mixer_kernel(p_ref, pa_ref, pq_ref, gt_ref, h_ref, sel_ref, conva_ref, dnconv_ref, alog_ref,
                  dtb_ref, dnw_ref, wo_ref, o_ref, y_ref, s_ref):
    r = MIX_ROWS
    pr = MIX_PAIR
    c = DN_CHUNK
    dh = DN_HEAD_DIM
    nb = MIX_BATCH
    t = pl.program_id(1)
    has_prev = t > 0

    @pl.when(t == 0)
    def _():
        s_ref[...] = jnp.zeros(s_ref.shape, F32)

    ii = lax.broadcasted_iota(jnp.int32, (pr, pr), 0)
    jj = lax.broadcasted_iota(jnp.int32, (pr, pr), 1)
    same = (ii >= c) == (jj >= c)
    incl = same & (ii >= jj)
    strict = same & (ii > jj)
    first = lax.broadcasted_iota(jnp.int32, (pr, 1), 0) < c
    row = t * r + lax.broadcasted_iota(jnp.int32, (r, LANES), 0)
    live = row >= PAD_ROWS
    sel = sel_ref[...]
    q0 = 3 * D_CONV

    keys = [(bb, pi, h) for bb in range(nb) for pi in range(MIX_PAIRS) for h in range(DN_HEADS)]
    local = {}
    amat = {}
    for bb in range(nb):
        prev_a = (pa_ref[bb, :, 0:D_CONV].astype(F32) * pa_ref[bb, :, 2 * D_CONV:3 * D_CONV].astype(F32))
        xa = jnp.concatenate(
            [jnp.where(has_prev, prev_a, 0.0),
             p_ref[bb, :, 0:D_CONV].astype(F32) * p_ref[bb, :, 2 * D_CONV:3 * D_CONV].astype(F32)], axis=0)
        ca = (conva_ref[0:1, :] * pltpu.roll(xa, 2, 0)[CONV_CARRY:]
              + conva_ref[1:2, :] * pltpu.roll(xa, 1, 0)[CONV_CARRY:]
              + conva_ref[2:3, :] * xa[CONV_CARRY:])
        y_ref[bb * r:(bb + 1) * r, 0:D_CONV] = (p_ref[bb, :, D_CONV:2 * D_CONV].astype(F32) * ca).astype(BF16)

        prev_q = jnp.where(has_prev, pq_ref[bb].astype(F32), 0.0)
        xe = jnp.concatenate([prev_q, p_ref[bb, :, q0:q0 + 3 * DN_DIM].astype(F32)], axis=0)
        cq = dnconv_ref[3:4, :] * xe[CONV_CARRY:]
        for j in range(3):
            cq = cq + dnconv_ref[j:j + 1, :] * pltpu.roll(xe, 3 - j, 0)[CONV_CARRY:]
        qkv = _silu(cq)

        gt = gt_ref[bb]
        beta_all = jnp.where(live, _sigmoid(gt), 0.0)
        xg = gt + dtb_ref[...]
        softplus = jnp.maximum(xg, 0.0) + jnp.log1p(jnp.exp(-jnp.abs(xg)))
        g_all = jnp.where(live, -jnp.exp(alog_ref[...]) * softplus, 0.0)
        g_hi = g_all.astype(BF16)
        g_r1 = g_all - g_hi.astype(F32)
        g_mid = g_r1.astype(BF16)
        g_lo = (g_r1 - g_mid.astype(F32)).astype(BF16)

        for pi in range(MIX_PAIRS):
            ps = slice(pi * pr, (pi + 1) * pr)
            cums = _dot(sel, g_hi[ps]) + (_dot(sel, g_mid[ps]) + _dot(sel, g_lo[ps]))
            cdec = cums[0:pr]
            dlast = tuple(jnp.broadcast_to(cums[pr + 8 * ci:pr + 8 * ci + 1], (pr, LANES)) for ci in range(2))
            cdec_t = cdec.T
            for h in range(DN_HEADS):
                q = qkv[ps, h * dh:(h + 1) * dh]
                k = qkv[ps, DN_DIM + h * dh:DN_DIM + (h + 1) * dh]
                v = qkv[ps, 2 * DN_DIM + h * dh:2 * DN_DIM + (h + 1) * dh]
                q = q * lax.rsqrt(jnp.sum(q * q, -1, keepdims=True) + EPS) * (dh ** -0.5)
                k = k * lax.rsqrt(jnp.sum(k * k, -1, keepdims=True) + EPS)
                gl = DN_HEADS + h
                beta = beta_all[ps, h:h + 1]
                cd_col = cdec[:, gl:gl + 1]
                cd_row = cdec_t[gl:gl + 1, :]
                dmask = jnp.where(incl, jnp.exp(jnp.where(incl, cd_col - cd_row, 0.0)), 0.0)
                kb = k.astype(BF16)
                qk_kk = _dot_nt(jnp.concatenate([q.astype(BF16), kb], axis=0), kb)
                amat[bb, pi, h] = jnp.where(strict, beta * qk_kk[pr:2 * pr] * dmask, 0.0)
                e_cd = jnp.exp(cd_col)
                dl_own = jnp.where(first, dlast[0][:, gl:gl + 1], dlast[1][:, gl:gl + 1])
                local[bb, pi, h] = dict(
                    rhs=jnp.concatenate([v * beta, k * (beta * e_cd)], axis=1),
                    qk=(qk_kk[0:pr] * dmask).astype(BF16),
                    q_dec=(q * e_cd).astype(BF16),
                    k_dec_t=(k * jnp.exp(dl_own - cd_col)).T.astype(BF16),
                    g_last=[jnp.exp(d[:, gl:gl + 1]) for d in dlast])

    pairs = [(bb, pi, hp) for bb in range(nb) for pi in range(MIX_PAIRS) for hp in range(DN_HEADS // 2)]
    tm = {key: -amat[key] for key in keys}
    pw = dict(amat)
    for _ in range(5):
        for bb, pi, hp in pairs:
            ka, kb_ = (bb, pi, 2 * hp), (bb, pi, 2 * hp + 1)
            pa_, pb_ = pw[ka].astype(BF16), pw[kb_].astype(BF16)
            sq = _dot(jnp.concatenate([pa_, pb_], axis=1), _blockdiag(pa_, pb_))
            pw[ka], pw[kb_] = sq[:, 0:pr], sq[:, pr:2 * pr]
        for bb, pi, hp in pairs:
            ka, kb_ = (bb, pi, 2 * hp), (bb, pi, 2 * hp + 1)
            prod = _dot(jnp.concatenate([tm[ka].astype(BF16), tm[kb_].astype(BF16)], axis=1),
                        _blockdiag(pw[ka].astype(BF16), pw[kb_].astype(BF16)))
            tm[ka] = tm[ka] + pw[ka] + prod[:, 0:pr]
            tm[kb_] = tm[kb_] + pw[kb_] + prod[:, pr:2 * pr]
    for key in keys:
        rhs = local[key]["rhs"]
        sol = rhs + _dot(tm[key].astype(BF16), rhs.astype(BF16))
        local[key]["u"] = sol[:, 0:dh]
        local[key]["w"] = sol[:, dh:2 * dh].astype(BF16)

    state = {(bb, h): s_ref[bb, h] for bb in range(nb) for h in range(DN_HEADS)}
    outs = {(bb, h): [] for bb in range(nb) for h in range(DN_HEADS)}
    zeros = jnp.zeros((c, dh), BF16)
    hpairs = [(bb, hp) for bb in range(nb) for hp in range(DN_HEADS // 2)]
    for pi in range(MIX_PAIRS):
        for ci in range(2):
            rs = slice(ci * c, (ci + 1) * c)
            ws_qs = {}
            for bb, hp in hpairs:
                lhs = jnp.concatenate(
                    [jnp.concatenate([local[bb, pi, h]["w"][rs], local[bb, pi, h]["q_dec"][rs]], axis=0)
                     for h in (2 * hp, 2 * hp + 1)], axis=1)
                ws_qs[bb, hp] = _dot(lhs, _blockdiag(state[bb, 2 * hp].astype(BF16),
                                                     state[bb, 2 * hp + 1].astype(BF16)))
            v_full = {}
            for bb in range(nb):
                for h in range(DN_HEADS):
                    ws = ws_qs[bb, h // 2][0:c, (h % 2) * dh:(h % 2 + 1) * dh]
                    vb = (local[bb, pi, h]["u"][rs] - ws).astype(BF16)
                    v_full[bb, h] = jnp.concatenate([vb, zeros] if ci == 0 else [zeros, vb], axis=0)
            for bb, hp in hpairs:
                ha, hb_ = 2 * hp, 2 * hp + 1
                upd = _dot(jnp.concatenate([local[bb, pi, ha]["k_dec_t"], local[bb, pi, hb_]["k_dec_t"]], axis=1),
                           _blockdiag(v_full[bb, ha], v_full[bb, hb_]))
                state[bb, ha] = state[bb, ha] * local[bb, pi, ha]["g_last"][ci] + upd[:, 0:dh]
                state[bb, hb_] = state[bb, hb_] * local[bb, pi, hb_]["g_last"][ci] + upd[:, dh:2 * dh]
            for bb, hp in hpairs:
                ha, hb_ = 2 * hp, 2 * hp + 1
                intra = _dot(jnp.concatenate([local[bb, pi, ha]["qk"][rs], local[bb, pi, hb_]["qk"][rs]], axis=1),
                             _blockdiag(v_full[bb, ha], v_full[bb, hb_]))
                outs[bb, ha].append(ws_qs[bb, hp][c:2 * c, 0:dh] + intra[:, 0:dh])
                outs[bb, hb_].append(ws_qs[bb, hp][c:2 * c, dh:2 * dh] + intra[:, dh:2 * dh])

    z0 = 3 * D_CONV + 3 * DN_DIM
    for bb in range(nb):
        for h in range(DN_HEADS):
            s_ref[bb, h] = state[bb, h]
            o = jnp.concatenate(outs[bb, h], axis=0)
            z = p_ref[bb, :, z0 + h * dh:z0 + (h + 1) * dh].astype(F32)
            y = _rms(o, dnw_ref[...]) * _silu(z)
            y_ref[bb * r:(bb + 1) * r, D_CONV + h * dh:D_CONV + (h + 1) * dh] = y.astype(BF16)

    mix = _dot(y_ref[...], wo_ref[...])
    for bb in range(nb):
        o_ref[bb] = h_ref[bb] + mix[bb * r:(bb + 1) * r]


def _decay_selectors():
    r, c = MIX_PAIR, DN_CHUNK
    i = jnp.arange(r)[:, None]
    j = jnp.arange(r)[None, :]
    tri = ((i // c) == (j // c)) & (i >= j)
    last0 = jnp.broadcast_to(j < c, (8, r))
    last1 = jnp.broadcast_to(j >= c, (8, r))
    return jnp.concatenate([tri, last0, last1], axis=0).astype(BF16)


def _mixer(p, gt, h, conv_a_w, dn_conv_w, a_log, dt_bias, dn_norm_w, w_out):
    b, lp, _ = p.shape
    d = h.shape[-1]
    r = MIX_ROWS
    nb = MIX_BATCH
    pad_lanes = lambda vec: jnp.zeros((1, LANES), F32).at[0, DN_HEADS:2 * DN_HEADS].set(vec.astype(F32))
    const = lambda shape: pl.BlockSpec(shape, lambda bi, t: (0,) * len(shape))
    prev_rows = lambda t: jnp.maximum(t * (r // CONV_CARRY) - 1, 0)
    qkv_cols = 3 * DN_DIM
    return pl.pallas_call(
        _mixer_kernel,
        grid=(b // nb, lp // r),
        in_specs=[pl.BlockSpec((nb, r, P_MAIN), lambda bi, t: (bi, t, 0)),
                  pl.BlockSpec((nb, CONV_CARRY, qkv_cols), lambda bi, t: (bi, prev_rows(t), 0)),
                  pl.BlockSpec((nb, CONV_CARRY, qkv_cols), lambda bi, t: (bi, prev_rows(t), 1)),
                  pl.BlockSpec((nb, r, LANES), lambda bi, t: (bi, t, 0)),
                  pl.BlockSpec((nb, r, d), lambda bi, t: (bi, t, 0)),
                  const((MIX_PAIR + 16, MIX_PAIR)),
                  const((3, D_CONV)),
                  const((4, 3 * DN_DIM)),
                  const((1, LANES)),
                  const((1, LANES)),
                  const((1, DN_HEAD_DIM)),
                  _resident((D_CONV + DN_DIM, d))],
        out_specs=pl.BlockSpec((nb, r, d), lambda bi, t: (bi, t, 0)),
        out_shape=jax.ShapeDtypeStruct((b, lp, d), F32),
        scratch_shapes=[pltpu.VMEM((nb * r, D_CONV + DN_DIM), BF16),
                        pltpu.VMEM((nb, DN_HEADS, DN_HEAD_DIM, DN_HEAD_DIM), F32)],
        compiler_params=_cparams(("arbitrary", "arbitrary")),
        name="gdn_mixer",
    )(p, p, p, gt, h, _decay_selectors(), conv_a_w.astype(F32), dn_conv_w.astype(F32),
      pad_lanes(a_log), pad_lanes(dt_bias), dn_norm_w.astype(F32).reshape(1, DN_HEAD_DIM),
      w_out.astype(BF16))


LOG2E = 1.4426950408889634
SWA_HALF = SWA_BLOCK // 2
SWA_KEYS = N_META + SWA_BLOCK + SWA_HALF


def _qkv_proj_kernel(x_ref, nw_ref, w_ref, hw_ref, o_ref):
    xn = _rms(x_ref[...], nw_ref[...]).astype(BF16)
    n_norm = hw_ref.shape[1]
    lo = lax.broadcasted_iota(jnp.int32, (1, LANES), 1) < SWA_HEAD_DIM
    for j in range(w_ref.shape[1] // MXU_COLS):
        yy = _dot(xn, w_ref[:, j * MXU_COLS:(j + 1) * MXU_COLS])
        for s in range(MXU_COLS // LANES):
            cs = slice(j * MXU_COLS + s * LANES, j * MXU_COLS + (s + 1) * LANES)
            y = yy[:, s * LANES:(s + 1) * LANES]
            if cs.start < n_norm:
                y2 = y * y
                ms_lo = jnp.sum(jnp.where(lo, y2, 0.0), -1, keepdims=True) * (1.0 / SWA_HEAD_DIM)
                ms_hi = jnp.sum(jnp.where(lo, 0.0, y2), -1, keepdims=True) * (1.0 / SWA_HEAD_DIM)
                y = y * jnp.where(lo, lax.rsqrt(ms_lo + EPS), lax.rsqrt(ms_hi + EPS)) * hw_ref[:, cs]
            o_ref[:, cs] = y.astype(o_ref.dtype)


def _qkv_proj(x, nw, w, hw, *, tm):
    m, k = x.shape
    n = w.shape[1]
    return pl.pallas_call(
        _qkv_proj_kernel,
        grid=(m // tm,),
        in_specs=[pl.BlockSpec((tm, k), lambda i: (i, 0)),
                  _resident((1, k)),
                  _resident((k, n)),
                  _resident((1, hw.shape[1]))],
        out_specs=pl.BlockSpec((tm, n), lambda i: (i, 0)),
        out_shape=jax.ShapeDtypeStruct((m, n), BF16),
        compiler_params=_cparams(("parallel",)),
        name="qkv_proj",
    )(x, nw, w, hw)


SWA_STEP_BLOCKS = 3


def _swa_kernel(sink_ref, q_ref, km_ref, kp_ref, kc_ref, vm_ref, vp_ref, vc_ref, h_ref, wo_ref, o_ref,
                att_ref):
    t = pl.program_id(1)
    blk = SWA_BLOCK
    hb = SWA_HALF
    nk = SWA_KEYS
    lo = lax.broadcasted_iota(jnp.int32, (1, LANES), 1) < SWA_HEAD_DIM

    def mask_bias(i, hf):
        r = hf * hb + lax.broadcasted_iota(jnp.int32, (hb, nk), 0)
        cidx = lax.broadcasted_iota(jnp.int32, (hb, nk), 1)
        n_prev = blk - hf * hb
        prev_j = cidx - N_META + hf * hb
        cur_j = cidx - N_META - n_prev
        meta_ok = (cidx < N_META) & ((i > 0) | (cidx <= r - PAD_ROWS))
        prev_ok = (cidx >= N_META) & (cidx < N_META + n_prev) & (prev_j > r) & (i >= 2)
        cur_ok = (cidx >= N_META + n_prev) & (cur_j <= r) & (i >= 1)
        bh = jnp.where(meta_ok | prev_ok | cur_ok, 0.0, NEG)
        return jnp.concatenate([bh] * SWA_GROUP, axis=0)

    def slab_rows(m_ref, p_ref, c_ref, slab):
        cs = slice(slab * LANES, (slab + 1) * LANES)
        full = jnp.concatenate([m_ref[0, :, cs], p_ref[0, :, cs], c_ref[0, :, cs]], axis=0)
        swapped = pltpu.roll(full, SWA_HEAD_DIM, 1)
        zero = jnp.zeros_like(full)
        even = (jnp.where(lo, full, zero), jnp.where(lo, zero, swapped))
        odd = (jnp.where(lo, swapped, zero), jnp.where(lo, zero, full))
        return even, odd

    def window(x, j, hf):
        a = N_META + j * blk + hf * hb
        return jnp.concatenate([x[0:N_META], x[a:a + nk - N_META]], axis=0)

    kslabs = [slab_rows(km_ref, kp_ref, kc_ref, s) for s in range(SWA_KV_HEADS // 2)]
    vslabs = [slab_rows(vm_ref, vp_ref, vc_ref, s) for s in range(SWA_KV_HEADS // 2)]
    bias = {(j, hf): mask_bias(t * SWA_STEP_BLOCKS + j, hf)
            for j in range(SWA_STEP_BLOCKS) for hf in range(2)}
    problems = [(j, kv, hf) for j in range(SWA_STEP_BLOCKS) for kv in range(SWA_KV_HEADS) for hf in range(2)]

    def logits(j, kv, hf):
        rows = slice(j * blk + hf * hb, j * blk + (hf + 1) * hb)
        base = kv * SWA_GROUP * SWA_HEAD_DIM
        lhs = jnp.concatenate([q_ref[0, rows, base:base + LANES],
                               q_ref[0, rows, base + LANES:base + 2 * LANES]], axis=0)
        k_lo, k_hi = kslabs[kv // 2][kv % 2]
        return jnp.concatenate([_dot_nt(lhs, window(k_lo, j, hf)), _dot_nt(lhs, window(k_hi, j, hf))],
                               axis=0) + bias[j, hf]

    grp = lax.broadcasted_iota(jnp.int32, (SWA_GROUP * hb, 1), 0) // hb
    sinks = []
    for kv in range(SWA_KV_HEADS):
        sink = jnp.zeros((SWA_GROUP * hb, 1), F32)
        for gi, g in enumerate((0, 2, 1, 3)):
            sink = jnp.where(grp == gi, sink_ref[kv * SWA_GROUP + g], sink)
        sinks.append(sink)

    def row_max(st, j, kv, hf):
        st["m"] = jnp.maximum(jnp.max(st["lg"], axis=-1, keepdims=True), sinks[kv])

    def exp_sum(st, j, kv, hf):
        e = jnp.exp2(st.pop("lg") - st["m"])
        st["den"] = jnp.sum(e, axis=-1, keepdims=True) + jnp.exp2(sinks[kv] - st.pop("m"))
        st["p"] = e.astype(BF16)

    def weighted_sum(st, j, kv, hf):
        p = st.pop("p")
        v_lo, v_hi = vslabs[kv // 2][kv % 2]
        st["o"] = (_dot(p[0:2 * hb], window(v_lo, j, hf))
                   + _dot(p[2 * hb:4 * hb], window(v_hi, j, hf)))
        st["rden"] = 1.0 / st.pop("den")

    def normalise(st, j, kv, hf):
        rden = st.pop("rden")
        o = st.pop("o") * jnp.where(lo, rden[0:2 * hb], rden[2 * hb:4 * hb])
        rows = slice(j * blk + hf * hb, j * blk + (hf + 1) * hb)
        base = kv * SWA_GROUP * SWA_HEAD_DIM
        att_ref[rows, base:base + LANES] = o[0:hb].astype(BF16)
        att_ref[rows, base + LANES:base + 2 * LANES] = o[hb:2 * hb].astype(BF16)

    stages = (None, row_max, exp_sum, weighted_sum, normalise)
    states = [dict() for _ in problems]
    for step in range(len(problems) + len(stages) - 1):
        for depth, stage in enumerate(stages):
            n = step - depth
            if 0 <= n < len(problems):
                if stage is None:
                    states[n]["lg"] = logits(*problems[n])
                else:
                    stage(states[n], *problems[n])

    o_ref[0] = h_ref[0] + _dot(att_ref[...], wo_ref[...])


def _swa(qkv, sinks, h, wo):
    b, lp, _ = qkv.shape
    d = h.shape[-1]
    blk = SWA_BLOCK
    rows = SWA_STEP_BLOCKS * blk
    nq = SWA_HEADS * SWA_HEAD_DIM
    nkv = SWA_KV_HEADS * SWA_HEAD_DIM
    kcol = nq // nkv
    vcol = kcol + 1
    meta_blk = PAD_ROWS // N_META
    prev = lambda t: jnp.maximum(t * SWA_STEP_BLOCKS - 1, 0)
    return pl.pallas_call(
        _swa_kernel,
        grid=(b, lp // rows),
        in_specs=[pl.BlockSpec(memory_space=pltpu.SMEM),
                  pl.BlockSpec((1, rows, nq), lambda bi, t: (bi, t, 0)),
                  pl.BlockSpec((1, N_META, nkv), lambda bi, t: (bi, meta_blk, kcol)),
                  pl.BlockSpec((1, blk, nkv), lambda bi, t: (bi, prev(t), kcol)),
                  pl.BlockSpec((1, rows, nkv), lambda bi, t: (bi, t, kcol)),
                  pl.BlockSpec((1, N_META, nkv), lambda bi, t: (bi, meta_blk, vcol)),
                  pl.BlockSpec((1, blk, nkv), lambda bi, t: (bi, prev(t), vcol)),
                  pl.BlockSpec((1, rows, nkv), lambda bi, t: (bi, t, vcol)),
                  pl.BlockSpec((1, rows, d), lambda bi, t: (bi, t, 0)),
                  _resident((nq, d))],
        out_specs=pl.BlockSpec((1, rows, d), lambda bi, t: (bi, t, 0)),
        out_shape=jax.ShapeDtypeStruct((b, lp, d), F32),
        scratch_shapes=[pltpu.VMEM((rows, nq), BF16)],
        compiler_params=_cparams(("parallel", "arbitrary")),
        name="swa",
    )((sinks.astype(F32) * LOG2E), qkv, qkv, qkv, qkv, qkv, qkv, qkv, h, wo.astype(BF16))


def kernel(x, meta_tokens, attn_norm_w, ffn_norm_w, mix_w_in, conv_a_w, dn_conv_w, dn_a_log,
           dn_dt_bias, dn_norm_w, mix_w_out, swa_wq, swa_wk, swa_wv, swa_q_norm_w, swa_k_norm_w,
           swa_sinks, swa_wo, ffn_w_up, ffn_conv_w, ffn_w_down):
    b, seq, d = x.shape
    lp = ROW0 + seq
    meta = jnp.broadcast_to(meta_tokens[None].astype(x.dtype), (b, N_META, d))
    h = jnp.concatenate([jnp.zeros((b, PAD_ROWS, d), x.dtype), meta, x], axis=1)
    depth = attn_norm_w.shape[0]
    for layer in range(depth):
        i = layer // 2
        nw = attn_norm_w[layer].reshape(1, d)
        hf = h.reshape(b * lp, d)
        if layer % 2 == 0:
            w_in = mix_w_in[i]
            wg = jnp.zeros((d, LANES), BF16).at[:, :2 * DN_HEADS].set(w_in[:, P_MAIN:].astype(BF16))
            p, gt = _in_proj(hf, nw, w_in[:, :P_MAIN].astype(BF16), wg, tm=1024)
            h = _mixer(p.reshape(b, lp, P_MAIN), gt.reshape(b, lp, LANES), h, conv_a_w[i], dn_conv_w[i],
                       dn_a_log[i], dn_dt_bias[i], dn_norm_w[i], mix_w_out[i])
        else:
            wqkv = jnp.concatenate([swa_wq[i], swa_wk[i], swa_wv[i]], axis=1).astype(BF16)
            head_w = jnp.concatenate([
                jnp.tile(swa_q_norm_w[i].astype(F32) * (SWA_HEAD_DIM ** -0.5 * LOG2E), SWA_HEADS),
                jnp.tile(swa_k_norm_w[i].astype(F32), SWA_KV_HEADS)]).reshape(1, -1)
            qkv = _qkv_proj(hf, nw, wqkv, head_w, tm=1024)
            h = _swa(qkv.reshape(b, lp, -1), swa_sinks[i], h, swa_wo[i])
        last = layer == depth - 1
        h = _ffn(h, ffn_norm_w[layer], ffn_w_up[layer], ffn_conv_w[layer], ffn_w_down[layer],
                 tm=512 if last else 528, first_row=ROW0 if last else 0)
    return h
```

```python
import functools

import jax
import jax.numpy as jnp
from jax import lax
from jax.experimental import pallas as pl
from jax.experimental.pallas import tpu as pltpu

F32 = jnp.float32
BF16 = jnp.bfloat16

N_META = 16
D_CONV = 512
DN_HEADS = 4
DN_HEAD_DIM = 128
DN_DIM = DN_HEADS * DN_HEAD_DIM
DN_CHUNK = 64
SWA_HEADS = 16
SWA_KV_HEADS = 4
SWA_GROUP = SWA_HEADS // SWA_KV_HEADS
SWA_HEAD_DIM = 64
SWA_BLOCK = 128
EPS = 1e-6

LANES = 128
ROW0 = 128
PAD_ROWS = ROW0 - N_META
P_MAIN = 3 * D_CONV + 4 * DN_DIM
NEG = -1e30
VMEM_LIMIT = 56 * 1024 * 1024


def _cparams(sem):
    return pltpu.CompilerParams(dimension_semantics=sem, vmem_limit_bytes=VMEM_LIMIT)


def _rms(x, w):
    ms = jnp.mean(x * x, axis=-1, keepdims=True)
    return x * lax.rsqrt(ms + EPS) * w


def _sigmoid(x):
    return 1.0 / (1.0 + jnp.exp(-x))


def _silu(x):
    return x * _sigmoid(x)


def _dot(a, b):
    return jnp.dot(a, b, preferred_element_type=F32)


def _dot_nt(a, b):
    return lax.dot_general(a, b, (((1,), (1,)), ((), ())), preferred_element_type=F32)


def _blockdiag(a, b):
    za = jnp.zeros_like(a)
    return jnp.concatenate([jnp.concatenate([a, za], axis=1), jnp.concatenate([za, b], axis=1)], axis=0)


MXU_COLS = 256


def _resident(shape):
    return pl.BlockSpec(shape, lambda *_: (0,) * len(shape), pipeline_mode=pl.Buffered(1))


TILE_BLOCKS = 3


def _tile_specs(nb, d, shift):
    return [pl.BlockSpec((nb, ROW0, d),
                         lambda bi, t, j=j: (bi, jnp.maximum(TILE_BLOCKS * t + j - shift, 0), 0))
            for j in range(TILE_BLOCKS)]


def _tile_rows(blk_refs, head_ref, bb, t, shift):
    blocks = [r[bb] for r in blk_refs]
    if shift:
        blocks[0] = jnp.where(t == 0, head_ref[...], blocks[0])
    return jnp.concatenate(blocks, axis=0)


def _in_proj_kernel(b0_ref, b1_ref, b2_ref, head_ref, nw_ref, w_ref, wg_ref, o_ref, og_ref, *, shift):
    x = _tile_rows((b0_ref, b1_ref, b2_ref), head_ref, 0, pl.program_id(1), shift)
    xn = _rms(x, nw_ref[...]).astype(BF16)
    for j in range(w_ref.shape[1] // MXU_COLS):
        cs = slice(j * MXU_COLS, (j + 1) * MXU_COLS)
        o_ref[0, :, cs] = _dot(xn, w_ref[:, cs]).astype(o_ref.dtype)
    og_ref[0] = _dot(xn, wg_ref[...])


def _in_proj(src, head, shift, nw, w, wg):
    b, rows, k = src.shape
    lp = rows + shift * ROW0
    n = w.shape[1]
    tm = TILE_BLOCKS * ROW0
    return pl.pallas_call(
        functools.partial(_in_proj_kernel, shift=shift),
        grid=(b, lp // tm),
        in_specs=_tile_specs(1, k, shift) + [
            _resident((ROW0, k)),
            _resident((1, k)),
            _resident((k, n)),
            _resident((k, LANES))],
        out_specs=[pl.BlockSpec((1, tm, n), lambda bi, t: (bi, t, 0)),
                   pl.BlockSpec((1, tm, LANES), lambda bi, t: (bi, t, 0))],
        out_shape=[jax.ShapeDtypeStruct((b, lp, n), BF16),
                   jax.ShapeDtypeStruct((b, lp, LANES), F32)],
        compiler_params=_cparams(("parallel", "arbitrary")),
        name="in_proj",
    )(src, src, src, head, nw, w, wg)


FFN_HALO = 16
FFN_CHUNK = 256


def _ffn_kernel(x_ref, xp_ref, nw_ref, wg_ref, wv_ref, cw_ref, wd_ref, o_ref, act_ref, *, seq_start):
    x = x_ref[...]
    xn = _rms(x, nw_ref[...]).astype(BF16)
    xp = xp_ref[...]
    if seq_start:
        xp = jnp.where(pl.program_id(1) > 0, xp, 0.0)
    xn_ext = jnp.concatenate([_rms(xp, nw_ref[...]).astype(BF16), xn], axis=0)
    for j in range(act_ref.shape[1] // FFN_CHUNK):
        cs = slice(j * FFN_CHUNK, (j + 1) * FFN_CHUNK)
        g = _dot(xn_ext, wg_ref[:, cs])
        v = _dot(xn, wv_ref[:, cs])
        g1 = pltpu.roll(g, 1, 0)[FFN_HALO:]
        g2 = pltpu.roll(g, 2, 0)[FFN_HALO:]
        conv = cw_ref[0:1, cs] * g2 + cw_ref[1:2, cs] * g1 + cw_ref[2:3, cs] * g[FFN_HALO:]
        act_ref[:, cs] = (_silu(conv) * v).astype(BF16)
    o_ref[...] = x + _dot(act_ref[...], wd_ref[...])


def _ffn(h, nw, w_up, conv_w, w_down, *, tm, first_row=0):
    b, lp, d = h.shape
    f = w_down.shape[0]
    rows = lp - first_row
    nt = rows // tm
    h2 = h.reshape(b * lp, d)
    if first_row == 0:
        x_spec = pl.BlockSpec((tm, d), lambda bi, t: (bi * nt + t, 0))
        halo_spec = pl.BlockSpec(
            (FFN_HALO, d), lambda bi, t: (jnp.maximum((bi * lp + t * tm) // FFN_HALO - 1, 0), 0))
    else:
        row0 = lambda bi, t: bi * lp + first_row + t * tm
        x_spec = pl.BlockSpec((pl.Element(tm), pl.Element(d)),
                              lambda bi, t: (pl.multiple_of(row0(bi, t), FFN_HALO), 0))
        halo_spec = pl.BlockSpec((pl.Element(FFN_HALO), pl.Element(d)),
                                 lambda bi, t: (pl.multiple_of(row0(bi, t) - FFN_HALO, FFN_HALO), 0))
    out = pl.pallas_call(
        functools.partial(_ffn_kernel, seq_start=first_row == 0),
        grid=(b, nt),
        in_specs=[x_spec,
                  halo_spec,
                  _resident((1, d)),
                  _resident((d, f)),
                  _resident((d, f)),
                  _resident((3, f)),
                  _resident((f, d))],
        out_specs=pl.BlockSpec((tm, d), lambda bi, t: (bi * nt + t, 0)),
        out_shape=jax.ShapeDtypeStruct((b * rows, d), F32),
        scratch_shapes=[pltpu.VMEM((tm, f), BF16)],
        compiler_params=_cparams(("parallel", "arbitrary")),
        name="ffn",
    )(h2, h2, nw.reshape(1, d), w_up[:, :f].astype(BF16), w_up[:, f:].astype(BF16),
      conv_w.astype(F32), w_down.astype(BF16))
    return out.reshape(b, rows, d)


MIX_PAIR = 2 * DN_CHUNK
MIX_PAIRS = TILE_BLOCKS
CONV_CARRY = 16
MIX_ROWS = MIX_PAIR * MIX_PAIRS
MIX_BATCH = 2


def _mixer_kernel(p_ref, pa_ref, pq_ref, gt_ref, hb0_ref, hb1_ref, hb2_ref, head_ref, sel_ref, conva_ref,
                  dnconv_ref, alog_ref, dtb_ref, dnw_ref, wo_ref, o_ref, y_ref, s_ref, *, shift):
    r = MIX_ROWS
    pr = MIX_PAIR
    c = DN_CHUNK
    dh = DN_HEAD_DIM
    nb = MIX_BATCH
    t = pl.program_id(1)
    has_prev = t > 0

    @pl.when(t == 0)
    def _():
        s_ref[...] = jnp.zeros(s_ref.shape, F32)

    ii = lax.broadcasted_iota(jnp.int32, (pr, pr), 0)
    jj = lax.broadcasted_iota(jnp.int32, (pr, pr), 1)
    same = (ii >= c) == (jj >= c)
    incl = same & (ii >= jj)
    strict = same & (ii > jj)
    first = lax.broadcasted_iota(jnp.int32, (pr, 1), 0) < c
    row = t * r + lax.broadcasted_iota(jnp.int32, (r, LANES), 0)
    live = row >= PAD_ROWS
    sel = sel_ref[...]
    q0 = 3 * D_CONV

    keys = [(bb, pi, h) for bb in range(nb) for pi in range(MIX_PAIRS) for h in range(DN_HEADS)]
    local = {}
    amat = {}
    for bb in range(nb):
        prev_a = (pa_ref[bb, :, 0:D_CONV].astype(F32) * pa_ref[bb, :, 2 * D_CONV:3 * D_CONV].astype(F32))
        xa = jnp.concatenate(
            [jnp.where(has_prev, prev_a, 0.0),
             p_ref[bb, :, 0:D_CONV].astype(F32) * p_ref[bb, :, 2 * D_CONV:3 * D_CONV].astype(F32)], axis=0)
        ca = (conva_ref[0:1, :] * pltpu.roll(xa, 2, 0)[CONV_CARRY:]
              + conva_ref[1:2, :] * pltpu.roll(xa, 1, 0)[CONV_CARRY:]
              + conva_ref[2:3, :] * xa[CONV_CARRY:])
        y_ref[bb * r:(bb + 1) * r, 0:D_CONV] = (p_ref[bb, :, D_CONV:2 * D_CONV].astype(F32) * ca).astype(BF16)

        prev_q = jnp.where(has_prev, pq_ref[bb].astype(F32), 0.0)
        xe = jnp.concatenate([prev_q, p_ref[bb, :, q0:q0 + 3 * DN_DIM].astype(F32)], axis=0)
        cq = dnconv_ref[3:4, :] * xe[CONV_CARRY:]
        for j in range(3):
            cq = cq + dnconv_ref[j:j + 1, :] * pltpu.roll(xe, 3 - j, 0)[CONV_CARRY:]
        qkv = _silu(cq)

        gt = gt_ref[bb]
        beta_all = jnp.where(live, _sigmoid(gt), 0.0)
        xg = gt + dtb_ref[...]
        softplus = jnp.maximum(xg, 0.0) + jnp.log1p(jnp.exp(-jnp.abs(xg)))
        g_all = jnp.where(live, -jnp.exp(alog_ref[...]) * softplus, 0.0)
        g_hi = g_all.astype(BF16)
        g_r1 = g_all - g_hi.astype(F32)
        g_mid = g_r1.astype(BF16)
        g_lo = (g_r1 - g_mid.astype(F32)).astype(BF16)

        for pi in range(MIX_PAIRS):
            ps = slice(pi * pr, (pi + 1) * pr)
            cums = _dot(sel, g_hi[ps]) + (_dot(sel, g_mid[ps]) + _dot(sel, g_lo[ps]))
            cdec = cums[0:pr]
            dlast = tuple(jnp.broadcast_to(cums[pr + 8 * ci:pr + 8 * ci + 1], (pr, LANES)) for ci in range(2))
            cdec_t = cdec.T
            for h in range(DN_HEADS):
                q = qkv[ps, h * dh:(h + 1) * dh]
                k = qkv[ps, DN_DIM + h * dh:DN_DIM + (h + 1) * dh]
                v = qkv[ps, 2 * DN_DIM + h * dh:2 * DN_DIM + (h + 1) * dh]
                q = q * lax.rsqrt(jnp.sum(q * q, -1, keepdims=True) + EPS) * (dh ** -0.5)
                k = k * lax.rsqrt(jnp.sum(k * k, -1, keepdims=True) + EPS)
                gl = DN_HEADS + h
                beta = beta_all[ps, h:h + 1]
                cd_col = cdec[:, gl:gl + 1]
                cd_row = cdec_t[gl:gl + 1, :]
                dmask = jnp.where(incl, jnp.exp(jnp.where(incl, cd_col - cd_row, 0.0)), 0.0)
                kb = k.astype(BF16)
                qk_kk = _dot_nt(jnp.concatenate([q.astype(BF16), kb], axis=0), kb)
                amat[bb, pi, h] = jnp.where(strict, beta * qk_kk[pr:2 * pr] * dmask, 0.0)
                e_cd = jnp.exp(cd_col)
                dl_own = jnp.where(first, dlast[0][:, gl:gl + 1], dlast[1][:, gl:gl + 1])
                local[bb, pi, h] = dict(
                    rhs=jnp.concatenate([v * beta, k * (beta * e_cd)], axis=1),
                    qk=(qk_kk[0:pr] * dmask).astype(BF16),
                    q_dec=(q * e_cd).astype(BF16),
                    k_dec_t=(k * jnp.exp(dl_own - cd_col)).T.astype(BF16),
                    g_last=[jnp.exp(d[:, gl:gl + 1]) for d in dlast])

    pairs = [(bb, pi, hp) for bb in range(nb) for pi in range(MIX_PAIRS) for hp in range(DN_HEADS // 2)]
    tm = {key: -amat[key] for key in keys}
    pw = dict(amat)
    for _ in range(5):
        for bb, pi, hp in pairs:
            ka, kb_ = (bb, pi, 2 * hp), (bb, pi, 2 * hp + 1)
            pa_, pb_ = pw[ka].astype(BF16), pw[kb_].astype(BF16)
            sq = _dot(jnp.concatenate([pa_, pb_], axis=1), _blockdiag(pa_, pb_))
            pw[ka], pw[kb_] = sq[:, 0:pr], sq[:, pr:2 * pr]
        for bb, pi, hp in pairs:
            ka, kb_ = (bb, pi, 2 * hp), (bb, pi, 2 * hp + 1)
            prod = _dot(jnp.concatenate([tm[ka].astype(BF16), tm[kb_].astype(BF16)], axis=1),
                        _blockdiag(pw[ka].astype(BF16), pw[kb_].astype(BF16)))
            tm[ka] = tm[ka] + pw[ka] + prod[:, 0:pr]
            tm[kb_] = tm[kb_] + pw[kb_] + prod[:, pr:2 * pr]
    for key in keys:
        rhs = local[key]["rhs"]
        sol = rhs + _dot(tm[key].astype(BF16), rhs.astype(BF16))
        local[key]["u"] = sol[:, 0:dh]
        local[key]["w"] = sol[:, dh:2 * dh].astype(BF16)

    state = {(bb, h): s_ref[bb, h] for bb in range(nb) for h in range(DN_HEADS)}
    outs = {(bb, h): [] for bb in range(nb) for h in range(DN_HEADS)}
    zeros = jnp.zeros((c, dh), BF16)
    hpairs = [(bb, hp) for bb in range(nb) for hp in range(DN_HEADS // 2)]
    for pi in range(MIX_PAIRS):
        for ci in range(2):
            rs = slice(ci * c, (ci + 1) * c)
            ws_qs = {}
            for bb, hp in hpairs:
                lhs = jnp.concatenate(
                    [jnp.concatenate([local[bb, pi, h]["w"][rs], local[bb, pi, h]["q_dec"][rs]], axis=0)
                     for h in (2 * hp, 2 * hp + 1)], axis=1)
                ws_qs[bb, hp] = _dot(lhs, _blockdiag(state[bb, 2 * hp].astype(BF16),
                                                     state[bb, 2 * hp + 1].astype(BF16)))
            v_full = {}
            for bb in range(nb):
                for h in range(DN_HEADS):
                    ws = ws_qs[bb, h // 2][0:c, (h % 2) * dh:(h % 2 + 1) * dh]
                    vb = (local[bb, pi, h]["u"][rs] - ws).astype(BF16)
                    v_full[bb, h] = jnp.concatenate([vb, zeros] if ci == 0 else [zeros, vb], axis=0)
            for bb, hp in hpairs:
                ha, hb_ = 2 * hp, 2 * hp + 1
                upd = _dot(jnp.concatenate([local[bb, pi, ha]["k_dec_t"], local[bb, pi, hb_]["k_dec_t"]], axis=1),
                           _blockdiag(v_full[bb, ha], v_full[bb, hb_]))
                state[bb, ha] = state[bb, ha] * local[bb, pi, ha]["g_last"][ci] + upd[:, 0:dh]
                state[bb, hb_] = state[bb, hb_] * local[bb, pi, hb_]["g_last"][ci] + upd[:, dh:2 * dh]
            for bb, hp in hpairs:
                ha, hb_ = 2 * hp, 2 * hp + 1
                intra = _dot(jnp.concatenate([local[bb, pi, ha]["qk"][rs], local[bb, pi, hb_]["qk"][rs]], axis=1),
                             _blockdiag(v_full[bb, ha], v_full[bb, hb_]))
                outs[bb, ha].append(ws_qs[bb, hp][c:2 * c, 0:dh] + intra[:, 0:dh])
                outs[bb, hb_].append(ws_qs[bb, hp][c:2 * c, dh:2 * dh] + intra[:, dh:2 * dh])

    z0 = 3 * D_CONV + 3 * DN_DIM
    for bb in range(nb):
        for h in range(DN_HEADS):
            s_ref[bb, h] = state[bb, h]
            o = jnp.concatenate(outs[bb, h], axis=0)
            z = p_ref[bb, :, z0 + h * dh:z0 + (h + 1) * dh].astype(F32)
            y = _rms(o, dnw_ref[...]) * _silu(z)
            y_ref[bb * r:(bb + 1) * r, D_CONV + h * dh:D_CONV + (h + 1) * dh] = y.astype(BF16)

    mix = _dot(y_ref[...], wo_ref[...])
    for bb in range(nb):
        res = _tile_rows((hb0_ref, hb1_ref, hb2_ref), head_ref, bb, t, shift)
        o_ref[bb] = res + mix[bb * r:(bb + 1) * r]


def _decay_selectors():
    r, c = MIX_PAIR, DN_CHUNK
    i = jnp.arange(r)[:, None]
    j = jnp.arange(r)[None, :]
    tri = ((i // c) == (j // c)) & (i >= j)
    last0 = jnp.broadcast_to(j < c, (8, r))
    last1 = jnp.broadcast_to(j >= c, (8, r))
    return jnp.concatenate([tri, last0, last1], axis=0).astype(BF16)


def _mixer(p, gt, src, head, shift, conv_a_w, dn_conv_w, a_log, dt_bias, dn_norm_w, w_out):
    b, lp, _ = p.shape
    d = src.shape[-1]
    r = MIX_ROWS
    nb = MIX_BATCH
    pad_lanes = lambda vec: jnp.zeros((1, LANES), F32).at[0, DN_HEADS:2 * DN_HEADS].set(vec.astype(F32))
    const = lambda shape: pl.BlockSpec(shape, lambda bi, t: (0,) * len(shape))
    prev_rows = lambda t: jnp.maximum(t * (r // CONV_CARRY) - 1, 0)
    qkv_cols = 3 * DN_DIM
    return pl.pallas_call(
        functools.partial(_mixer_kernel, shift=shift),
        grid=(b // nb, lp // r),
        in_specs=[pl.BlockSpec((nb, r, P_MAIN), lambda bi, t: (bi, t, 0)),
                  pl.BlockSpec((nb, CONV_CARRY, qkv_cols), lambda bi, t: (bi, prev_rows(t), 0)),
                  pl.BlockSpec((nb, CONV_CARRY, qkv_cols), lambda bi, t: (bi, prev_rows(t), 1)),
                  pl.BlockSpec((nb, r, LANES), lambda bi, t: (bi, t, 0))]
                 + _tile_specs(nb, d, shift) + [
                  const((ROW0, d)),
                  const((MIX_PAIR + 16, MIX_PAIR)),
                  const((3, D_CONV)),
                  const((4, 3 * DN_DIM)),
                  const((1, LANES)),
                  const((1, LANES)),
                  const((1, DN_HEAD_DIM)),
                  _resident((D_CONV + DN_DIM, d))],
        out_specs=pl.BlockSpec((nb, r, d), lambda bi, t: (bi, t, 0)),
        out_shape=jax.ShapeDtypeStruct((b, lp, d), F32),
        scratch_shapes=[pltpu.VMEM((nb * r, D_CONV + DN_DIM), BF16),
                        pltpu.VMEM((nb, DN_HEADS, DN_HEAD_DIM, DN_HEAD_DIM), F32)],
        compiler_params=_cparams(("arbitrary", "arbitrary")),
        name="gdn_mixer",
    )(p, p, p, gt, src, src, src, head, _decay_selectors(), conv_a_w.astype(F32), dn_conv_w.astype(F32),
      pad_lanes(a_log), pad_lanes(dt_bias), dn_norm_w.astype(F32).reshape(1, DN_HEAD_DIM),
      w_out.astype(BF16))


LOG2E = 1.4426950408889634
SWA_HALF = SWA_BLOCK // 2
SWA_KEYS = N_META + SWA_BLOCK + SWA_HALF


def _qkv_proj_kernel(x_ref, nw_ref, w_ref, hw_ref, o_ref):
    xn = _rms(x_ref[...], nw_ref[...]).astype(BF16)
    n_norm = hw_ref.shape[1]
    lo = lax.broadcasted_iota(jnp.int32, (1, LANES), 1) < SWA_HEAD_DIM
    for j in range(w_ref.shape[1] // MXU_COLS):
        yy = _dot(xn, w_ref[:, j * MXU_COLS:(j + 1) * MXU_COLS])
        for s in range(MXU_COLS // LANES):
            cs = slice(j * MXU_COLS + s * LANES, j * MXU_COLS + (s + 1) * LANES)
            y = yy[:, s * LANES:(s + 1) * LANES]
            if cs.start < n_norm:
                y2 = y * y
                ms_lo = jnp.sum(jnp.where(lo, y2, 0.0), -1, keepdims=True) * (1.0 / SWA_HEAD_DIM)
                ms_hi = jnp.sum(jnp.where(lo, 0.0, y2), -1, keepdims=True) * (1.0 / SWA_HEAD_DIM)
                y = y * jnp.where(lo, lax.rsqrt(ms_lo + EPS), lax.rsqrt(ms_hi + EPS)) * hw_ref[:, cs]
            o_ref[:, cs] = y.astype(o_ref.dtype)


def _qkv_proj(x, nw, w, hw, *, tm):
    m, k = x.shape
    n = w.shape[1]
    return pl.pallas_call(
        _qkv_proj_kernel,
        grid=(m // tm,),
        in_specs=[pl.BlockSpec((tm, k), lambda i: (i, 0)),
                  _resident((1, k)),
                  _resident((k, n)),
                  _resident((1, hw.shape[1]))],
        out_specs=pl.BlockSpec((tm, n), lambda i: (i, 0)),
        out_shape=jax.ShapeDtypeStruct((m, n), BF16),
        compiler_params=_cparams(("parallel",)),
        name="qkv_proj",
    )(x, nw, w, hw)


SWA_STEP_BLOCKS = 3


def _swa_kernel(sink_ref, q_ref, km_ref, kp_ref, kc_ref, vm_ref, vp_ref, vc_ref, h_ref, wo_ref, o_ref,
                att_ref):
    t = pl.program_id(1)
    blk = SWA_BLOCK
    hb = SWA_HALF
    nk = SWA_KEYS
    lo = lax.broadcasted_iota(jnp.int32, (1, LANES), 1) < SWA_HEAD_DIM

    def mask_bias(i, hf):
        r = hf * hb + lax.broadcasted_iota(jnp.int32, (hb, nk), 0)
        cidx = lax.broadcasted_iota(jnp.int32, (hb, nk), 1)
        n_prev = blk - hf * hb
        prev_j = cidx - N_META + hf * hb
        cur_j = cidx - N_META - n_prev
        meta_ok = (cidx < N_META) & ((i > 0) | (cidx <= r - PAD_ROWS))
        prev_ok = (cidx >= N_META) & (cidx < N_META + n_prev) & (prev_j > r) & (i >= 2)
        cur_ok = (cidx >= N_META + n_prev) & (cur_j <= r) & (i >= 1)
        bh = jnp.where(meta_ok | prev_ok | cur_ok, 0.0, NEG)
        return jnp.concatenate([bh] * SWA_GROUP, axis=0)

    def slab_rows(m_ref, p_ref, c_ref, slab):
        cs = slice(slab * LANES, (slab + 1) * LANES)
        full = jnp.concatenate([m_ref[0, :, cs], p_ref[0, :, cs], c_ref[0, :, cs]], axis=0)
        swapped = pltpu.roll(full, SWA_HEAD_DIM, 1)
        zero = jnp.zeros_like(full)
        even = (jnp.where(lo, full, zero), jnp.where(lo, zero, swapped))
        odd = (jnp.where(lo, swapped, zero), jnp.where(lo, zero, full))
        return even, odd

    def window(x, j, hf):
        a = N_META + j * blk + hf * hb
        return jnp.concatenate([x[0:N_META], x[a:a + nk - N_META]], axis=0)

    kslabs = [slab_rows(km_ref, kp_ref, kc_ref, s) for s in range(SWA_KV_HEADS // 2)]
    vslabs = [slab_rows(vm_ref, vp_ref, vc_ref, s) for s in range(SWA_KV_HEADS // 2)]
    bias = {(j, hf): mask_bias(t * SWA_STEP_BLOCKS + j, hf)
            for j in range(SWA_STEP_BLOCKS) for hf in range(2)}
    problems = [(j, kv, hf) for j in range(SWA_STEP_BLOCKS) for kv in range(SWA_KV_HEADS) for hf in range(2)]

    def logits(j, kv, hf):
        rows = slice(j * blk + hf * hb, j * blk + (hf + 1) * hb)
        base = kv * SWA_GROUP * SWA_HEAD_DIM
        lhs = jnp.concatenate([q_ref[0, rows, base:base + LANES],
                               q_ref[0, rows, base + LANES:base + 2 * LANES]], axis=0)
        k_lo, k_hi = kslabs[kv // 2][kv % 2]
        return jnp.concatenate([_dot_nt(lhs, window(k_lo, j, hf)), _dot_nt(lhs, window(k_hi, j, hf))],
                               axis=0) + bias[j, hf]

    grp = lax.broadcasted_iota(jnp.int32, (SWA_GROUP * hb, 1), 0) // hb
    sinks = []
    for kv in range(SWA_KV_HEADS):
        sink = jnp.zeros((SWA_GROUP * hb, 1), F32)
        for gi, g in enumerate((0, 2, 1, 3)):
            sink = jnp.where(grp == gi, sink_ref[kv * SWA_GROUP + g], sink)
        sinks.append(sink)

    def row_max(st, j, kv, hf):
        st["m"] = jnp.maximum(jnp.max(st["lg"], axis=-1, keepdims=True), sinks[kv])

    def exp_sum(st, j, kv, hf):
        e = jnp.exp2(st.pop("lg") - st["m"])
        st["den"] = jnp.sum(e, axis=-1, keepdims=True) + jnp.exp2(sinks[kv] - st.pop("m"))
        st["p"] = e.astype(BF16)

    def weighted_sum(st, j, kv, hf):
        p = st.pop("p")
        v_lo, v_hi = vslabs[kv // 2][kv % 2]
        st["o"] = (_dot(p[0:2 * hb], window(v_lo, j, hf))
                   + _dot(p[2 * hb:4 * hb], window(v_hi, j, hf)))
        st["rden"] = 1.0 / st.pop("den")

    def normalise(st, j, kv, hf):
        rden = st.pop("rden")
        o = st.pop("o") * jnp.where(lo, rden[0:2 * hb], rden[2 * hb:4 * hb])
        rows = slice(j * blk + hf * hb, j * blk + (hf + 1) * hb)
        base = kv * SWA_GROUP * SWA_HEAD_DIM
        att_ref[rows, base:base + LANES] = o[0:hb].astype(BF16)
        att_ref[rows, base + LANES:base + 2 * LANES] = o[hb:2 * hb].astype(BF16)

    stages = (None, row_max, exp_sum, weighted_sum, normalise)
    states = [dict() for _ in problems]
    for step in range(len(problems) + len(stages) - 1):
        for depth, stage in enumerate(stages):
            n = step - depth
            if 0 <= n < len(problems):
                if stage is None:
                    states[n]["lg"] = logits(*problems[n])
                else:
                    stage(states[n], *problems[n])

    o_ref[0] = h_ref[0] + _dot(att_ref[...], wo_ref[...])


def _swa(qkv, sinks, h, wo):
    b, lp, _ = qkv.shape
    d = h.shape[-1]
    blk = SWA_BLOCK
    rows = SWA_STEP_BLOCKS * blk
    nq = SWA_HEADS * SWA_HEAD_DIM
    nkv = SWA_KV_HEADS * SWA_HEAD_DIM
    kcol = nq // nkv
    vcol = kcol + 1
    meta_blk = PAD_ROWS // N_META
    prev = lambda t: jnp.maximum(t * SWA_STEP_BLOCKS - 1, 0)
    return pl.pallas_call(
        _swa_kernel,
        grid=(b, lp // rows),
        in_specs=[pl.BlockSpec(memory_space=pltpu.SMEM),
                  pl.BlockSpec((1, rows, nq), lambda bi, t: (bi, t, 0)),
                  pl.BlockSpec((1, N_META, nkv), lambda bi, t: (bi, meta_blk, kcol)),
                  pl.BlockSpec((1, blk, nkv), lambda bi, t: (bi, prev(t), kcol)),
                  pl.BlockSpec((1, rows, nkv), lambda bi, t: (bi, t, kcol)),
                  pl.BlockSpec((1, N_META, nkv), lambda bi, t: (bi, meta_blk, vcol)),
                  pl.BlockSpec((1, blk, nkv), lambda bi, t: (bi, prev(t), vcol)),
                  pl.BlockSpec((1, rows, nkv), lambda bi, t: (bi, t, vcol)),
                  pl.BlockSpec((1, rows, d), lambda bi, t: (bi, t, 0)),
                  _resident((nq, d))],
        out_specs=pl.BlockSpec((1, rows, d), lambda bi, t: (bi, t, 0)),
        out_shape=jax.ShapeDtypeStruct((b, lp, d), F32),
        scratch_shapes=[pltpu.VMEM((rows, nq), BF16)],
        compiler_params=_cparams(("parallel", "arbitrary")),
        name="swa",
    )((sinks.astype(F32) * LOG2E), qkv, qkv, qkv, qkv, qkv, qkv, qkv, h, wo.astype(BF16))


def kernel(x, meta_tokens, attn_norm_w, ffn_norm_w, mix_w_in, conv_a_w, dn_conv_w, dn_a_log,
           dn_dt_bias, dn_norm_w, mix_w_out, swa_wq, swa_wk, swa_wv, swa_q_norm_w, swa_k_norm_w,
           swa_sinks, swa_wo, ffn_w_up, ffn_conv_w, ffn_w_down):
    b, seq, d = x.shape
    lp = ROW0 + seq
    head = jnp.concatenate([jnp.zeros((PAD_ROWS, d), x.dtype), meta_tokens.astype(x.dtype)], axis=0)
    h = None
    depth = attn_norm_w.shape[0]
    for layer in range(depth):
        i = layer // 2
        nw = attn_norm_w[layer].reshape(1, d)
        src, shift = (x, 1) if h is None else (h, 0)
        if layer % 2 == 0:
            w_in = mix_w_in[i]
            wg = jnp.zeros((d, LANES), BF16).at[:, :2 * DN_HEADS].set(w_in[:, P_MAIN:].astype(BF16))
            p, gt = _in_proj(src, head, shift, nw, w_in[:, :P_MAIN].astype(BF16), wg)
            h = _mixer(p, gt, src, head, shift, conv_a_w[i], dn_conv_w[i],
                       dn_a_log[i], dn_dt_bias[i], dn_norm_w[i], mix_w_out[i])
        else:
            wqkv = jnp.concatenate([swa_wq[i], swa_wk[i], swa_wv[i]], axis=1).astype(BF16)
            head_w = jnp.concatenate([
                jnp.tile(swa_q_norm_w[i].astype(F32) * (SWA_HEAD_DIM ** -0.5 * LOG2E), SWA_HEADS),
                jnp.tile(swa_k_norm_w[i].astype(F32), SWA_KV_HEADS)]).reshape(1, -1)
            qkv = _qkv_proj(h.reshape(b * lp, d), nw, wqkv, head_w, tm=1024)
            h = _swa(qkv.reshape(b, lp, -1), swa_sinks[i], h, swa_wo[i])
        last = layer == depth - 1
        h = _ffn(h, ffn_norm_w[layer], ffn_w_up[layer], ffn_conv_w[layer], ffn_w_down[layer],
                 tm=1024 if last else 704, first_row=ROW0 if last else 0)
    return h
```

```python
import functools

import jax
import jax.numpy as jnp
from jax import lax
from jax.experimental import pallas as pl
from jax.experimental.pallas import tpu as pltpu

F32 = jnp.float32
BF16 = jnp.bfloat16

N_META = 16
D_CONV = 512
DN_HEADS = 4
DN_HEAD_DIM = 128
DN_DIM = DN_HEADS * DN_HEAD_DIM
DN_CHUNK = 64
SWA_HEADS = 16
SWA_KV_HEADS = 4
SWA_GROUP = SWA_HEADS // SWA_KV_HEADS
SWA_HEAD_DIM = 64
SWA_BLOCK = 128
EPS = 1e-6

LANES = 128
ROW0 = 128
PAD_ROWS = ROW0 - N_META
P_MAIN = 3 * D_CONV + 4 * DN_DIM
NEG = -1e30
VMEM_LIMIT = 56 * 1024 * 1024


def _cparams(sem):
    return pltpu.CompilerParams(dimension_semantics=sem, vmem_limit_bytes=VMEM_LIMIT)


def _rms(x, w):
    ms = jnp.mean(x * x, axis=-1, keepdims=True)
    return x * lax.rsqrt(ms + EPS) * w


def _sigmoid(x):
    return 1.0 / (1.0 + jnp.exp(-x))


def _silu(x):
    return x * _sigmoid(x)


def _dot(a, b):
    return jnp.dot(a, b, preferred_element_type=F32)


def _dot_nt(a, b):
    return lax.dot_general(a, b, (((1,), (1,)), ((), ())), preferred_element_type=F32)


def _blockdiag(a, b):
    za = jnp.zeros_like(a)
    return jnp.concatenate([jnp.concatenate([a, za], axis=1), jnp.concatenate([za, b], axis=1)], axis=0)


MXU_COLS = 256


def _resident(shape):
    return pl.BlockSpec(shape, lambda *_: (0,) * len(shape), pipeline_mode=pl.Buffered(1))


TILE_BLOCKS = 3


def _tile_specs(nb, d, shift):
    return [pl.BlockSpec((nb, ROW0, d),
                         lambda bi, t, j=j: (bi, jnp.maximum(TILE_BLOCKS * t + j - shift, 0), 0))
            for j in range(TILE_BLOCKS)]


def _tile_rows(blk_refs, head_ref, bb, t, shift):
    blocks = [r[bb] for r in blk_refs]
    if shift:
        blocks[0] = jnp.where(t == 0, head_ref[...], blocks[0])
    return jnp.concatenate(blocks, axis=0)


def _in_proj_kernel(b0_ref, b1_ref, b2_ref, head_ref, nw_ref, w_ref, wg_ref, o_ref, og_ref, *, shift):
    x = _tile_rows((b0_ref, b1_ref, b2_ref), head_ref, 0, pl.program_id(1), shift)
    xn = _rms(x, nw_ref[...]).astype(BF16)
    for j in range(w_ref.shape[1] // MXU_COLS):
        cs = slice(j * MXU_COLS, (j + 1) * MXU_COLS)
        o_ref[0, :, cs] = _dot(xn, w_ref[:, cs]).astype(o_ref.dtype)
    og_ref[0] = _dot(xn, wg_ref[...])


def _in_proj(src, head, shift, nw, w, wg):
    b, rows, k = src.shape
    lp = rows + shift * ROW0
    n = w.shape[1]
    tm = TILE_BLOCKS * ROW0
    return pl.pallas_call(
        functools.partial(_in_proj_kernel, shift=shift),
        grid=(b, lp // tm),
        in_specs=_tile_specs(1, k, shift) + [
            _resident((ROW0, k)),
            _resident((1, k)),
            _resident((k, n)),
            _resident((k, LANES))],
        out_specs=[pl.BlockSpec((1, tm, n), lambda bi, t: (bi, t, 0)),
                   pl.BlockSpec((1, tm, LANES), lambda bi, t: (bi, t, 0))],
        out_shape=[jax.ShapeDtypeStruct((b, lp, n), BF16),
                   jax.ShapeDtypeStruct((b, lp, LANES), F32)],
        compiler_params=_cparams(("parallel", "arbitrary")),
        name="in_proj",
    )(src, src, src, head, nw, w, wg)


FFN_HALO = 16
FFN_CHUNK = 256


def _ffn_kernel(x_ref, xp_ref, nw_ref, wg_ref, wv_ref, cw_ref, wd_ref, o_ref, act_ref, *, seq_start):
    x = x_ref[...]
    xn = _rms(x, nw_ref[...]).astype(BF16)
    xp = xp_ref[...]
    if seq_start:
        xp = jnp.where(pl.program_id(1) > 0, xp, 0.0)
    xn_ext = jnp.concatenate([_rms(xp, nw_ref[...]).astype(BF16), xn], axis=0)
    for j in range(act_ref.shape[1] // FFN_CHUNK):
        cs = slice(j * FFN_CHUNK, (j + 1) * FFN_CHUNK)
        g = _dot(xn_ext, wg_ref[:, cs])
        v = _dot(xn, wv_ref[:, cs])
        g1 = pltpu.roll(g, 1, 0)[FFN_HALO:]
        g2 = pltpu.roll(g, 2, 0)[FFN_HALO:]
        conv = cw_ref[0:1, cs] * g2 + cw_ref[1:2, cs] * g1 + cw_ref[2:3, cs] * g[FFN_HALO:]
        act_ref[:, cs] = (_silu(conv) * v).astype(BF16)
    o_ref[...] = x + _dot(act_ref[...], wd_ref[...])


def _ffn(h, nw, w_up, conv_w, w_down, *, tm, first_row=0):
    b, lp, d = h.shape
    f = w_down.shape[0]
    rows = lp - first_row
    nt = rows // tm
    h2 = h.reshape(b * lp, d)
    if first_row == 0:
        x_spec = pl.BlockSpec((tm, d), lambda bi, t: (bi * nt + t, 0))
        halo_spec = pl.BlockSpec(
            (FFN_HALO, d), lambda bi, t: (jnp.maximum((bi * lp + t * tm) // FFN_HALO - 1, 0), 0))
    else:
        row0 = lambda bi, t: bi * lp + first_row + t * tm
        x_spec = pl.BlockSpec((pl.Element(tm), pl.Element(d)),
                              lambda bi, t: (pl.multiple_of(row0(bi, t), FFN_HALO), 0))
        halo_spec = pl.BlockSpec((pl.Element(FFN_HALO), pl.Element(d)),
                                 lambda bi, t: (pl.multiple_of(row0(bi, t) - FFN_HALO, FFN_HALO), 0))
    out = pl.pallas_call(
        functools.partial(_ffn_kernel, seq_start=first_row == 0),
        grid=(b, nt),
        in_specs=[x_spec,
                  halo_spec,
                  _resident((1, d)),
                  _resident((d, f)),
                  _resident((d, f)),
                  _resident((3, f)),
                  _resident((f, d))],
        out_specs=pl.BlockSpec((tm, d), lambda bi, t: (bi * nt + t, 0)),
        out_shape=jax.ShapeDtypeStruct((b * rows, d), F32),
        scratch_shapes=[pltpu.VMEM((tm, f), BF16)],
        compiler_params=_cparams(("parallel", "arbitrary")),
        name="ffn",
    )(h2, h2, nw.reshape(1, d), w_up[:, :f].astype(BF16), w_up[:, f:].astype(BF16),
      conv_w.astype(F32), w_down.astype(BF16))
    return out.reshape(b, rows, d)


MIX_PAIR = 2 * DN_CHUNK
MIX_PAIRS = TILE_BLOCKS
CONV_CARRY = 16
MIX_ROWS = MIX_PAIR * MIX_PAIRS
MIX_BATCH = 2


def _mixer_kernel(p_ref, pa_ref, pq_ref, gt_ref, hb0_ref, hb1_ref, hb2_ref, head_ref, sel_ref, conva_ref,
                  dnconv_ref, alog_ref, dtb_ref, dnw_ref, wo_ref, o_ref, y_ref, s_ref, *, shift):
    r = MIX_ROWS
    pr = MIX_PAIR
    c = DN_CHUNK
    dh = DN_HEAD_DIM
    nb = MIX_BATCH
    t = pl.program_id(1)
    has_prev = t > 0

    @pl.when(t == 0)
    def _():
        s_ref[...] = jnp.zeros(s_ref.shape, F32)

    ii = lax.broadcasted_iota(jnp.int32, (pr, pr), 0)
    jj = lax.broadcasted_iota(jnp.int32, (pr, pr), 1)
    same = (ii >= c) == (jj >= c)
    incl = same & (ii >= jj)
    strict = same & (ii > jj)
    first = lax.broadcasted_iota(jnp.int32, (pr, 1), 0) < c
    row = t * r + lax.broadcasted_iota(jnp.int32, (r, LANES), 0)
    live = row >= PAD_ROWS
    sel = sel_ref[...]
    q0 = 3 * D_CONV

    keys = [(bb, pi, h) for bb in range(nb) for pi in range(MIX_PAIRS) for h in range(DN_HEADS)]
    local = {}
    amat = {}
    for bb in range(nb):
        prev_a = (pa_ref[bb, :, 0:D_CONV].astype(F32) * pa_ref[bb, :, 2 * D_CONV:3 * D_CONV].astype(F32))
        xa = jnp.concatenate(
            [jnp.where(has_prev, prev_a, 0.0),
             p_ref[bb, :, 0:D_CONV].astype(F32) * p_ref[bb, :, 2 * D_CONV:3 * D_CONV].astype(F32)], axis=0)
        ca = (conva_ref[0:1, :] * pltpu.roll(xa, 2, 0)[CONV_CARRY:]
              + conva_ref[1:2, :] * pltpu.roll(xa, 1, 0)[CONV_CARRY:]
              + conva_ref[2:3, :] * xa[CONV_CARRY:])
        y_ref[bb * r:(bb + 1) * r, 0:D_CONV] = (p_ref[bb, :, D_CONV:2 * D_CONV].astype(F32) * ca).astype(BF16)

        prev_q = jnp.where(has_prev, pq_ref[bb].astype(F32), 0.0)
        xe = jnp.concatenate([prev_q, p_ref[bb, :, q0:q0 + 3 * DN_DIM].astype(F32)], axis=0)
        cq = dnconv_ref[3:4, :] * xe[CONV_CARRY:]
        for j in range(3):
            cq = cq + dnconv_ref[j:j + 1, :] * pltpu.roll(xe, 3 - j, 0)[CONV_CARRY:]
        qkv = _silu(cq)

        gt = gt_ref[bb]
        beta_all = jnp.where(live, _sigmoid(gt), 0.0)
        xg = gt + dtb_ref[...]
        softplus = jnp.maximum(xg, 0.0) + jnp.log1p(jnp.exp(-jnp.abs(xg)))
        g_all = jnp.where(live, -jnp.exp(alog_ref[...]) * softplus, 0.0)
        g_hi = g_all.astype(BF16)
        g_r1 = g_all - g_hi.astype(F32)
        g_mid = g_r1.astype(BF16)
        g_lo = (g_r1 - g_mid.astype(F32)).astype(BF16)

        for pi in range(MIX_PAIRS):
            ps = slice(pi * pr, (pi + 1) * pr)
            cums = _dot(sel, g_hi[ps]) + (_dot(sel, g_mid[ps]) + _dot(sel, g_lo[ps]))
            cdec = cums[0:pr]
            dlast = tuple(jnp.broadcast_to(cums[pr + 8 * ci:pr + 8 * ci + 1], (pr, LANES)) for ci in range(2))
            cdec_t = cdec.T
            for h in range(DN_HEADS):
                q = qkv[ps, h * dh:(h + 1) * dh]
                k = qkv[ps, DN_DIM + h * dh:DN_DIM + (h + 1) * dh]
                v = qkv[ps, 2 * DN_DIM + h * dh:2 * DN_DIM + (h + 1) * dh]
                q = q * lax.rsqrt(jnp.sum(q * q, -1, keepdims=True) + EPS) * (dh ** -0.5)
                k = k * lax.rsqrt(jnp.sum(k * k, -1, keepdims=True) + EPS)
                gl = DN_HEADS + h
                beta = beta_all[ps, h:h + 1]
                cd_col = cdec[:, gl:gl + 1]
                cd_row = cdec_t[gl:gl + 1, :]
                dmask = jnp.where(incl, jnp.exp(jnp.where(incl, cd_col - cd_row, 0.0)), 0.0)
                kb = k.astype(BF16)
                qk_kk = _dot_nt(jnp.concatenate([q.astype(BF16), kb], axis=0), kb)
                amat[bb, pi, h] = jnp.where(strict, beta * qk_kk[pr:2 * pr] * dmask, 0.0)
                e_cd = jnp.exp(cd_col)
                dl_own = jnp.where(first, dlast[0][:, gl:gl + 1], dlast[1][:, gl:gl + 1])
                local[bb, pi, h] = dict(
                    rhs=jnp.concatenate([v * beta, k * (beta * e_cd)], axis=1),
                    qk=(qk_kk[0:pr] * dmask).astype(BF16),
                    q_dec=(q * e_cd).astype(BF16),
                    k_dec_t=(k * jnp.exp(dl_own - cd_col)).T.astype(BF16),
                    g_last=[jnp.exp(d[:, gl:gl + 1]) for d in dlast])

    pairs = [(bb, pi, hp) for bb in range(nb) for pi in range(MIX_PAIRS) for hp in range(DN_HEADS // 2)]
    tm = {key: -amat[key] for key in keys}
    pw = dict(amat)
    for _ in range(5):
        for bb, pi, hp in pairs:
            ka, kb_ = (bb, pi, 2 * hp), (bb, pi, 2 * hp + 1)
            pa_, pb_ = pw[ka].astype(BF16), pw[kb_].astype(BF16)
            sq = _dot(jnp.concatenate([pa_, pb_], axis=1), _blockdiag(pa_, pb_))
            pw[ka], pw[kb_] = sq[:, 0:pr], sq[:, pr:2 * pr]
        for bb, pi, hp in pairs:
            ka, kb_ = (bb, pi, 2 * hp), (bb, pi, 2 * hp + 1)
            prod = _dot(jnp.concatenate([tm[ka].astype(BF16), tm[kb_].astype(BF16)], axis=1),
                        _blockdiag(pw[ka].astype(BF16), pw[kb_].astype(BF16)))
            tm[ka] = tm[ka] + pw[ka] + prod[:, 0:pr]
            tm[kb_] = tm[kb_] + pw[kb_] + prod[:, pr:2 * pr]
    for key in keys:
        rhs = local[key]["rhs"]
        sol = rhs + _dot(tm[key].astype(BF16), rhs.astype(BF16))
        local[key]["u"] = sol[:, 0:dh]
        local[key]["w"] = sol[:, dh:2 * dh].astype(BF16)

    state = {(bb, h): s_ref[bb, h] for bb in range(nb) for h in range(DN_HEADS)}
    outs = {(bb, h): [] for bb in range(nb) for h in range(DN_HEADS)}
    zeros = jnp.zeros((c, dh), BF16)
    hpairs = [(bb, hp) for bb in range(nb) for hp in range(DN_HEADS // 2)]
    for pi in range(MIX_PAIRS):
        for ci in range(2):
            rs = slice(ci * c, (ci + 1) * c)
            ws_qs = {}
            for bb, hp in hpairs:
                lhs = jnp.concatenate(
                    [jnp.concatenate([local[bb, pi, h]["w"][rs], local[bb, pi, h]["q_dec"][rs]], axis=0)
                     for h in (2 * hp, 2 * hp + 1)], axis=1)
                ws_qs[bb, hp] = _dot(lhs, _blockdiag(state[bb, 2 * hp].astype(BF16),
                                                     state[bb, 2 * hp + 1].astype(BF16)))
            v_full = {}
            for bb in range(nb):
                for h in range(DN_HEADS):
                    ws = ws_qs[bb, h // 2][0:c, (h % 2) * dh:(h % 2 + 1) * dh]
                    vb = (local[bb, pi, h]["u"][rs] - ws).astype(BF16)
                    v_full[bb, h] = jnp.concatenate([vb, zeros] if ci == 0 else [zeros, vb], axis=0)
            for bb, hp in hpairs:
                ha, hb_ = 2 * hp, 2 * hp + 1
                upd = _dot(jnp.concatenate([local[bb, pi, ha]["k_dec_t"], local[bb, pi, hb_]["k_dec_t"]], axis=1),
                           _blockdiag(v_full[bb, ha], v_full[bb, hb_]))
                state[bb, ha] = state[bb, ha] * local[bb, pi, ha]["g_last"][ci] + upd[:, 0:dh]
                state[bb, hb_] = state[bb, hb_] * local[bb, pi, hb_]["g_last"][ci] + upd[:, dh:2 * dh]
            for bb, hp in hpairs:
                ha, hb_ = 2 * hp, 2 * hp + 1
                intra = _dot(jnp.concatenate([local[bb, pi, ha]["qk"][rs], local[bb, pi, hb_]["qk"][rs]], axis=1),
                             _blockdiag(v_full[bb, ha], v_full[bb, hb_]))
                outs[bb, ha].append(ws_qs[bb, hp][c:2 * c, 0:dh] + intra[:, 0:dh])
                outs[bb, hb_].append(ws_qs[bb, hp][c:2 * c, dh:2 * dh] + intra[:, dh:2 * dh])

    z0 = 3 * D_CONV + 3 * DN_DIM
    for bb in range(nb):
        for h in range(DN_HEADS):
            s_ref[bb, h] = state[bb, h]
            o = jnp.concatenate(outs[bb, h], axis=0)
            z = p_ref[bb, :, z0 + h * dh:z0 + (h + 1) * dh].astype(F32)
            y = _rms(o, dnw_ref[...]) * _silu(z)
            y_ref[bb * r:(bb + 1) * r, D_CONV + h * dh:D_CONV + (h + 1) * dh] = y.astype(BF16)

    mix = _dot(y_ref[...], wo_ref[...])
    for bb in range(nb):
        res = _tile_rows((hb0_ref, hb1_ref, hb2_ref), head_ref, bb, t, shift)
        o_ref[bb] = res + mix[bb * r:(bb + 1) * r]


def _decay_selectors():
    r, c = MIX_PAIR, DN_CHUNK
    i = jnp.arange(r)[:, None]
    j = jnp.arange(r)[None, :]
    tri = ((i // c) == (j // c)) & (i >= j)
    last0 = jnp.broadcast_to(j < c, (8, r))
    last1 = jnp.broadcast_to(j >= c, (8, r))
    return jnp.concatenate([tri, last0, last1], axis=0).astype(BF16)


def _mixer(p, gt, src, head, shift, conv_a_w, dn_conv_w, a_log, dt_bias, dn_norm_w, w_out):
    b, lp, _ = p.shape
    d = src.shape[-1]
    r = MIX_ROWS
    nb = MIX_BATCH
    pad_lanes = lambda vec: jnp.zeros((1, LANES), F32).at[0, DN_HEADS:2 * DN_HEADS].set(vec.astype(F32))
    const = lambda shape: pl.BlockSpec(shape, lambda bi, t: (0,) * len(shape))
    prev_rows = lambda t: jnp.maximum(t * (r // CONV_CARRY) - 1, 0)
    qkv_cols = 3 * DN_DIM
    return pl.pallas_call(
        functools.partial(_mixer_kernel, shift=shift),
        grid=(b // nb, lp // r),
        in_specs=[pl.BlockSpec((nb, r, P_MAIN), lambda bi, t: (bi, t, 0)),
                  pl.BlockSpec((nb, CONV_CARRY, qkv_cols), lambda bi, t: (bi, prev_rows(t), 0)),
                  pl.BlockSpec((nb, CONV_CARRY, qkv_cols), lambda bi, t: (bi, prev_rows(t), 1)),
                  pl.BlockSpec((nb, r, LANES), lambda bi, t: (bi, t, 0))]
                 + _tile_specs(nb, d, shift) + [
                  const((ROW0, d)),
                  const((MIX_PAIR + 16, MIX_PAIR)),
                  const((3, D_CONV)),
                  const((4, 3 * DN_DIM)),
                  const((1, LANES)),
                  const((1, LANES)),
                  const((1, DN_HEAD_DIM)),
                  _resident((D_CONV + DN_DIM, d))],
        out_specs=pl.BlockSpec((nb, r, d), lambda bi, t: (bi, t, 0)),
        out_shape=jax.ShapeDtypeStruct((b, lp, d), F32),
        scratch_shapes=[pltpu.VMEM((nb * r, D_CONV + DN_DIM), BF16),
                        pltpu.VMEM((nb, DN_HEADS, DN_HEAD_DIM, DN_HEAD_DIM), F32)],
        compiler_params=_cparams(("arbitrary", "arbitrary")),
        name="gdn_mixer",
    )(p, p, p, gt, src, src, src, head, _decay_selectors(), conv_a_w.astype(F32), dn_conv_w.astype(F32),
      pad_lanes(a_log), pad_lanes(dt_bias), dn_norm_w.astype(F32).reshape(1, DN_HEAD_DIM),
      w_out.astype(BF16))


LOG2E = 1.4426950408889634
SWA_HALF = SWA_BLOCK // 2
SWA_KEYS = N_META + SWA_BLOCK + SWA_HALF


def _qkv_proj_kernel(x_ref, nw_ref, w_ref, hw_ref, o_ref):
    xn = _rms(x_ref[...], nw_ref[...]).astype(BF16)
    n_norm = hw_ref.shape[1]
    lo = lax.broadcasted_iota(jnp.int32, (1, LANES), 1) < SWA_HEAD_DIM
    for j in range(w_ref.shape[1] // MXU_COLS):
        yy = _dot(xn, w_ref[:, j * MXU_COLS:(j + 1) * MXU_COLS])
        for s in range(MXU_COLS // LANES):
            cs = slice(j * MXU_COLS + s * LANES, j * MXU_COLS + (s + 1) * LANES)
            y = yy[:, s * LANES:(s + 1) * LANES]
            if cs.start < n_norm:
                y2 = y * y
                ms_lo = jnp.sum(jnp.where(lo, y2, 0.0), -1, keepdims=True) * (1.0 / SWA_HEAD_DIM)
                ms_hi = jnp.sum(jnp.where(lo, 0.0, y2), -1, keepdims=True) * (1.0 / SWA_HEAD_DIM)
                y = y * jnp.where(lo, lax.rsqrt(ms_lo + EPS), lax.rsqrt(ms_hi + EPS)) * hw_ref[:, cs]
            o_ref[:, cs] = y.astype(o_ref.dtype)


def _qkv_proj(x, nw, w, hw, *, tm):
    m, k = x.shape
    n = w.shape[1]
    return pl.pallas_call(
        _qkv_proj_kernel,
        grid=(m // tm,),
        in_specs=[pl.BlockSpec((tm, k), lambda i: (i, 0)),
                  _resident((1, k)),
                  _resident((k, n)),
                  _resident((1, hw.shape[1]))],
        out_specs=pl.BlockSpec((tm, n), lambda i: (i, 0)),
        out_shape=jax.ShapeDtypeStruct((m, n), BF16),
        compiler_params=_cparams(("parallel",)),
        name="qkv_proj",
    )(x, nw, w, hw)


SWA_STEP_BLOCKS = 3
SWA_BATCH = 2


def _swa_kernel(sink_ref, q_ref, km_ref, kp_ref, kc_ref, vm_ref, vp_ref, vc_ref, h_ref, wo_ref, o_ref,
                att_ref):
    t = pl.program_id(1)
    blk = SWA_BLOCK
    hb = SWA_HALF
    nk = SWA_KEYS
    lo = lax.broadcasted_iota(jnp.int32, (1, LANES), 1) < SWA_HEAD_DIM

    def mask_bias(i, hf):
        r = hf * hb + lax.broadcasted_iota(jnp.int32, (hb, nk), 0)
        cidx = lax.broadcasted_iota(jnp.int32, (hb, nk), 1)
        n_prev = blk - hf * hb
        prev_j = cidx - N_META + hf * hb
        cur_j = cidx - N_META - n_prev
        meta_ok = (cidx < N_META) & ((i > 0) | (cidx <= r - PAD_ROWS))
        prev_ok = (cidx >= N_META) & (cidx < N_META + n_prev) & (prev_j > r) & (i >= 2)
        cur_ok = (cidx >= N_META + n_prev) & (cur_j <= r) & (i >= 1)
        bh = jnp.where(meta_ok | prev_ok | cur_ok, 0.0, NEG)
        return jnp.concatenate([bh] * SWA_GROUP, axis=0)

    def slab_rows(m_ref, p_ref, c_ref, bb, slab):
        cs = slice(slab * LANES, (slab + 1) * LANES)
        full = jnp.concatenate([m_ref[bb, :, cs], p_ref[bb, :, cs], c_ref[bb, :, cs]], axis=0)
        swapped = pltpu.roll(full, SWA_HEAD_DIM, 1)
        zero = jnp.zeros_like(full)
        even = (jnp.where(lo, full, zero), jnp.where(lo, zero, swapped))
        odd = (jnp.where(lo, swapped, zero), jnp.where(lo, zero, full))
        return even, odd

    def window(x, j, hf):
        a = N_META + j * blk + hf * hb
        return jnp.concatenate([x[0:N_META], x[a:a + nk - N_META]], axis=0)

    seqs = range(SWA_BATCH)
    kslabs = [[slab_rows(km_ref, kp_ref, kc_ref, bb, s) for s in range(SWA_KV_HEADS // 2)] for bb in seqs]
    vslabs = [[slab_rows(vm_ref, vp_ref, vc_ref, bb, s) for s in range(SWA_KV_HEADS // 2)] for bb in seqs]
    bias = {(j, hf): mask_bias(t * SWA_STEP_BLOCKS + j, hf)
            for j in range(SWA_STEP_BLOCKS) for hf in range(2)}
    problems = [(bb, j, kv, hf) for j in range(SWA_STEP_BLOCKS) for kv in range(SWA_KV_HEADS)
                for hf in range(2) for bb in seqs]

    def logits(bb, j, kv, hf):
        rows = slice(j * blk + hf * hb, j * blk + (hf + 1) * hb)
        base = kv * SWA_GROUP * SWA_HEAD_DIM
        lhs = jnp.concatenate([q_ref[bb, rows, base:base + LANES],
                               q_ref[bb, rows, base + LANES:base + 2 * LANES]], axis=0)
        k_lo, k_hi = kslabs[bb][kv // 2][kv % 2]
        return jnp.concatenate([_dot_nt(lhs, window(k_lo, j, hf)), _dot_nt(lhs, window(k_hi, j, hf))],
                               axis=0) + bias[j, hf]

    grp = lax.broadcasted_iota(jnp.int32, (SWA_GROUP * hb, 1), 0) // hb
    sinks = []
    for kv in range(SWA_KV_HEADS):
        sink = jnp.zeros((SWA_GROUP * hb, 1), F32)
        for gi, g in enumerate((0, 2, 1, 3)):
            sink = jnp.where(grp == gi, sink_ref[kv * SWA_GROUP + g], sink)
        sinks.append(sink)

    def row_max(st, bb, j, kv, hf):
        st["m"] = jnp.maximum(jnp.max(st["lg"], axis=-1, keepdims=True), sinks[kv])

    def exp_sum(st, bb, j, kv, hf):
        e = jnp.exp2(st.pop("lg") - st["m"])
        st["den"] = jnp.sum(e, axis=-1, keepdims=True) + jnp.exp2(sinks[kv] - st.pop("m"))
        st["p"] = e.astype(BF16)

    def weighted_sum(st, bb, j, kv, hf):
        p = st.pop("p")
        v_lo, v_hi = vslabs[bb][kv // 2][kv % 2]
        st["o"] = (_dot(p[0:2 * hb], window(v_lo, j, hf))
                   + _dot(p[2 * hb:4 * hb], window(v_hi, j, hf)))
        st["rden"] = 1.0 / st.pop("den")

    def normalise(st, bb, j, kv, hf):
        rden = st.pop("rden")
        o = st.pop("o") * jnp.where(lo, rden[0:2 * hb], rden[2 * hb:4 * hb])
        r0 = bb * SWA_STEP_BLOCKS * blk + j * blk + hf * hb
        rows = slice(r0, r0 + hb)
        base = kv * SWA_GROUP * SWA_HEAD_DIM
        att_ref[rows, base:base + LANES] = o[0:hb].astype(BF16)
        att_ref[rows, base + LANES:base + 2 * LANES] = o[hb:2 * hb].astype(BF16)

    stages = (None, row_max, exp_sum, weighted_sum, normalise)
    states = [dict() for _ in problems]
    for step in range(len(problems) + len(stages) - 1):
        for depth, stage in enumerate(stages):
            n = step - depth
            if 0 <= n < len(problems):
                if stage is None:
                    states[n]["lg"] = logits(*problems[n])
                else:
                    stage(states[n], *problems[n])

    proj = _dot(att_ref[...], wo_ref[...])
    step_rows = SWA_STEP_BLOCKS * blk
    for bb in seqs:
        o_ref[bb] = h_ref[bb] + proj[bb * step_rows:(bb + 1) * step_rows]


def _swa(qkv, sinks, h, wo):
    b, lp, _ = qkv.shape
    d = h.shape[-1]
    blk = SWA_BLOCK
    rows = SWA_STEP_BLOCKS * blk
    nq = SWA_HEADS * SWA_HEAD_DIM
    nkv = SWA_KV_HEADS * SWA_HEAD_DIM
    kcol = nq // nkv
    vcol = kcol + 1
    meta_blk = PAD_ROWS // N_META
    prev = lambda t: jnp.maximum(t * SWA_STEP_BLOCKS - 1, 0)
    nb = SWA_BATCH
    return pl.pallas_call(
        _swa_kernel,
        grid=(b // nb, lp // rows),
        in_specs=[pl.BlockSpec(memory_space=pltpu.SMEM),
                  pl.BlockSpec((nb, rows, nq), lambda bi, t: (bi, t, 0)),
                  pl.BlockSpec((nb, N_META, nkv), lambda bi, t: (bi, meta_blk, kcol)),
                  pl.BlockSpec((nb, blk, nkv), lambda bi, t: (bi, prev(t), kcol)),
                  pl.BlockSpec((nb, rows, nkv), lambda bi, t: (bi, t, kcol)),
                  pl.BlockSpec((nb, N_META, nkv), lambda bi, t: (bi, meta_blk, vcol)),
                  pl.BlockSpec((nb, blk, nkv), lambda bi, t: (bi, prev(t), vcol)),
                  pl.BlockSpec((nb, rows, nkv), lambda bi, t: (bi, t, vcol)),
                  pl.BlockSpec((nb, rows, d), lambda bi, t: (bi, t, 0)),
                  _resident((nq, d))],
        out_specs=pl.BlockSpec((nb, rows, d), lambda bi, t: (bi, t, 0)),
        out_shape=jax.ShapeDtypeStruct((b, lp, d), F32),
        scratch_shapes=[pltpu.VMEM((nb * rows, nq), BF16)],
        compiler_params=_cparams(("parallel", "arbitrary")),
        name="swa",
    )((sinks.astype(F32) * LOG2E), qkv, qkv, qkv, qkv, qkv, qkv, qkv, h, wo.astype(BF16))


def kernel(x, meta_tokens, attn_norm_w, ffn_norm_w, mix_w_in, conv_a_w, dn_conv_w, dn_a_log,
           dn_dt_bias, dn_norm_w, mix_w_out, swa_wq, swa_wk, swa_wv, swa_q_norm_w, swa_k_norm_w,
           swa_sinks, swa_wo, ffn_w_up, ffn_conv_w, ffn_w_down):
    b, seq, d = x.shape
    lp = ROW0 + seq
    head = jnp.concatenate([jnp.zeros((PAD_ROWS, d), x.dtype), meta_tokens.astype(x.dtype)], axis=0)
    h = None
    depth = attn_norm_w.shape[0]
    for layer in range(depth):
        i = layer // 2
        nw = attn_norm_w[layer].reshape(1, d)
        src, shift = (x, 1) if h is None else (h, 0)
        if layer % 2 == 0:
            w_in = mix_w_in[i]
            wg = jnp.zeros((d, LANES), BF16).at[:, :2 * DN_HEADS].set(w_in[:, P_MAIN:].astype(BF16))
            p, gt = _in_proj(src, head, shift, nw, w_in[:, :P_MAIN].astype(BF16), wg)
            h = _mixer(p, gt, src, head, shift, conv_a_w[i], dn_conv_w[i],
                       dn_a_log[i], dn_dt_bias[i], dn_norm_w[i], mix_w_out[i])
        else:
            wqkv = jnp.concatenate([swa_wq[i], swa_wk[i], swa_wv[i]], axis=1).astype(BF16)
            head_w = jnp.concatenate([
                jnp.tile(swa_q_norm_w[i].astype(F32) * (SWA_HEAD_DIM ** -0.5 * LOG2E), SWA_HEADS),
                jnp.tile(swa_k_norm_w[i].astype(F32), SWA_KV_HEADS)]).reshape(1, -1)
            qkv = _qkv_proj(h.reshape(b * lp, d), nw, wqkv, head_w, tm=1024)
            h = _swa(qkv.reshape(b, lp, -1), swa_sinks[i], h, swa_wo[i])
        last = layer == depth - 1
        h = _ffn(h, ffn_norm_w[layer], ffn_w_up[layer], ffn_conv_w[layer], ffn_w_down[layer],
                 tm=1024 if last else 704, first_row=ROW0 if last else 0)
    return h
```

```python
import functools

import jax
import jax.numpy as jnp
from jax import lax
from jax.experimental import pallas as pl
from jax.experimental.pallas import tpu as pltpu

F32 = jnp.float32
BF16 = jnp.bfloat16

N_META = 16
D_CONV = 512
DN_HEADS = 4
DN_HEAD_DIM = 128
DN_DIM = DN_HEADS * DN_HEAD_DIM
DN_CHUNK = 64
SWA_HEADS = 16
SWA_KV_HEADS = 4
SWA_GROUP = SWA_HEADS // SWA_KV_HEADS
SWA_HEAD_DIM = 64
SWA_BLOCK = 128
EPS = 1e-6

LANES = 128
ROW0 = 128
PAD_ROWS = ROW0 - N_META
P_MAIN = 3 * D_CONV + 4 * DN_DIM
NEG = -1e30
VMEM_LIMIT = 56 * 1024 * 1024


def _cparams(sem):
    return pltpu.CompilerParams(dimension_semantics=sem, vmem_limit_bytes=VMEM_LIMIT)


def _rms(x, w):
    ms = jnp.mean(x * x, axis=-1, keepdims=True)
    return x * lax.rsqrt(ms + EPS) * w


def _sigmoid(x):
    return 1.0 / (1.0 + jnp.exp(-x))


def _silu(x):
    return x * _sigmoid(x)


def _dot(a, b):
    return jnp.dot(a, b, preferred_element_type=F32)


def _dot_nt(a, b):
    return lax.dot_general(a, b, (((1,), (1,)), ((), ())), preferred_element_type=F32)


def _blockdiag(a, b):
    za = jnp.zeros_like(a)
    return jnp.concatenate([jnp.concatenate([a, za], axis=1), jnp.concatenate([za, b], axis=1)], axis=0)


MXU_COLS = 256


def _resident(shape):
    return pl.BlockSpec(shape, lambda *_: (0,) * len(shape), pipeline_mode=pl.Buffered(1))


TILE_BLOCKS = 3


def _tile_specs(nb, d, shift):
    return [pl.BlockSpec((nb, ROW0, d),
                         lambda bi, t, j=j: (bi, jnp.maximum(TILE_BLOCKS * t + j - shift, 0), 0))
            for j in range(TILE_BLOCKS)]


def _tile_rows(blk_refs, head_ref, bb, t, shift):
    blocks = [r[bb] for r in blk_refs]
    if shift:
        blocks[0] = jnp.where(t == 0, head_ref[...], blocks[0])
    return jnp.concatenate(blocks, axis=0)


def _in_proj_kernel(b0_ref, b1_ref, b2_ref, head_ref, nw_ref, w_ref, wg_ref, o_ref, og_ref, *, shift):
    x = _tile_rows((b0_ref, b1_ref, b2_ref), head_ref, 0, pl.program_id(1), shift)
    xn = _rms(x, nw_ref[...]).astype(BF16)
    for j in range(w_ref.shape[1] // MXU_COLS):
        cs = slice(j * MXU_COLS, (j + 1) * MXU_COLS)
        o_ref[0, :, cs] = _dot(xn, w_ref[:, cs]).astype(o_ref.dtype)
    og_ref[0] = _dot(xn, wg_ref[...])


def _in_proj(src, head, shift, nw, w, wg):
    b, rows, k = src.shape
    lp = rows + shift * ROW0
    n = w.shape[1]
    tm = TILE_BLOCKS * ROW0
    return pl.pallas_call(
        functools.partial(_in_proj_kernel, shift=shift),
        grid=(b, lp // tm),
        in_specs=_tile_specs(1, k, shift) + [
            _resident((ROW0, k)),
            _resident((1, k)),
            _resident((k, n)),
            _resident((k, LANES))],
        out_specs=[pl.BlockSpec((1, tm, n), lambda bi, t: (bi, t, 0)),
                   pl.BlockSpec((1, tm, LANES), lambda bi, t: (bi, t, 0))],
        out_shape=[jax.ShapeDtypeStruct((b, lp, n), BF16),
                   jax.ShapeDtypeStruct((b, lp, LANES), F32)],
        compiler_params=_cparams(("parallel", "arbitrary")),
        name="in_proj",
    )(src, src, src, head, nw, w, wg)


FFN_HALO = 16
FFN_CHUNK = 256


def _ffn_kernel(x_ref, xp_ref, nw_ref, wg_ref, wv_ref, cw_ref, wd_ref, o_ref, act_ref, *, seq_start):
    x = x_ref[...]
    xn = _rms(x, nw_ref[...]).astype(BF16)
    xp = xp_ref[...]
    if seq_start:
        xp = jnp.where(pl.program_id(1) > 0, xp, 0.0)
    xn_ext = jnp.concatenate([_rms(xp, nw_ref[...]).astype(BF16), xn], axis=0)
    for j in range(act_ref.shape[1] // FFN_CHUNK):
        cs = slice(j * FFN_CHUNK, (j + 1) * FFN_CHUNK)
        g = _dot(xn_ext, wg_ref[:, cs])
        v = _dot(xn, wv_ref[:, cs])
        g1 = pltpu.roll(g, 1, 0)[FFN_HALO:]
        g2 = pltpu.roll(g, 2, 0)[FFN_HALO:]
        conv = cw_ref[0:1, cs] * g2 + cw_ref[1:2, cs] * g1 + cw_ref[2:3, cs] * g[FFN_HALO:]
        act_ref[:, cs] = (_silu(conv) * v).astype(BF16)
    o_ref[...] = x + _dot(act_ref[...], wd_ref[...])


def _ffn(h, nw, w_up, conv_w, w_down, *, tm, first_row=0):
    b, lp, d = h.shape
    f = w_down.shape[0]
    rows = lp - first_row
    nt = rows // tm
    h2 = h.reshape(b * lp, d)
    if first_row == 0:
        x_spec = pl.BlockSpec((tm, d), lambda bi, t: (bi * nt + t, 0))
        halo_spec = pl.BlockSpec(
            (FFN_HALO, d), lambda bi, t: (jnp.maximum((bi * lp + t * tm) // FFN_HALO - 1, 0), 0))
    else:
        row0 = lambda bi, t: bi * lp + first_row + t * tm
        x_spec = pl.BlockSpec((pl.Element(tm), pl.Element(d)),
                              lambda bi, t: (pl.multiple_of(row0(bi, t), FFN_HALO), 0))
        halo_spec = pl.BlockSpec((pl.Element(FFN_HALO), pl.Element(d)),
                                 lambda bi, t: (pl.multiple_of(row0(bi, t) - FFN_HALO, FFN_HALO), 0))
    out = pl.pallas_call(
        functools.partial(_ffn_kernel, seq_start=first_row == 0),
        grid=(b, nt),
        in_specs=[x_spec,
                  halo_spec,
                  _resident((1, d)),
                  _resident((d, f)),
                  _resident((d, f)),
                  _resident((3, f)),
                  _resident((f, d))],
        out_specs=pl.BlockSpec((tm, d), lambda bi, t: (bi * nt + t, 0)),
        out_shape=jax.ShapeDtypeStruct((b * rows, d), F32),
        scratch_shapes=[pltpu.VMEM((tm, f), BF16)],
        compiler_params=_cparams(("parallel", "arbitrary")),
        name="ffn",
    )(h2, h2, nw.reshape(1, d), w_up[:, :f].astype(BF16), w_up[:, f:].astype(BF16),
      conv_w.astype(F32), w_down.astype(BF16))
    return out.reshape(b, rows, d)


MIX_PAIR = 2 * DN_CHUNK
MIX_PAIRS = TILE_BLOCKS
CONV_CARRY = 16
MIX_ROWS = MIX_PAIR * MIX_PAIRS
MIX_BATCH = 2
SOLVE_BASE = 8


def _mixer_kernel(p_ref, pa_ref, pq_ref, gt_ref, hb0_ref, hb1_ref, hb2_ref, head_ref, sel_ref, conva_ref,
                  dnconv_ref, alog_ref, dtb_ref, dnw_ref, wo_ref, o_ref, y_ref, s_ref, *, shift):
    r = MIX_ROWS
    pr = MIX_PAIR
    c = DN_CHUNK
    dh = DN_HEAD_DIM
    nb = MIX_BATCH
    t = pl.program_id(1)
    has_prev = t > 0

    @pl.when(t == 0)
    def _():
        s_ref[...] = jnp.zeros(s_ref.shape, F32)

    ii = lax.broadcasted_iota(jnp.int32, (pr, pr), 0)
    jj = lax.broadcasted_iota(jnp.int32, (pr, pr), 1)
    same = (ii >= c) == (jj >= c)
    incl = same & (ii >= jj)
    strict = same & (ii > jj)
    first = lax.broadcasted_iota(jnp.int32, (pr, 1), 0) < c
    row = t * r + lax.broadcasted_iota(jnp.int32, (r, LANES), 0)
    live = row >= PAD_ROWS
    sel = sel_ref[...]
    q0 = 3 * D_CONV

    keys = [(bb, pi, h) for bb in range(nb) for pi in range(MIX_PAIRS) for h in range(DN_HEADS)]
    local = {}
    amat = {}
    for bb in range(nb):
        prev_a = (pa_ref[bb, :, 0:D_CONV].astype(F32) * pa_ref[bb, :, 2 * D_CONV:3 * D_CONV].astype(F32))
        xa = jnp.concatenate(
            [jnp.where(has_prev, prev_a, 0.0),
             p_ref[bb, :, 0:D_CONV].astype(F32) * p_ref[bb, :, 2 * D_CONV:3 * D_CONV].astype(F32)], axis=0)
        ca = (conva_ref[0:1, :] * pltpu.roll(xa, 2, 0)[CONV_CARRY:]
              + conva_ref[1:2, :] * pltpu.roll(xa, 1, 0)[CONV_CARRY:]
              + conva_ref[2:3, :] * xa[CONV_CARRY:])
        y_ref[bb * r:(bb + 1) * r, 0:D_CONV] = (p_ref[bb, :, D_CONV:2 * D_CONV].astype(F32) * ca).astype(BF16)

        prev_q = jnp.where(has_prev, pq_ref[bb].astype(F32), 0.0)
        xe = jnp.concatenate([prev_q, p_ref[bb, :, q0:q0 + 3 * DN_DIM].astype(F32)], axis=0)
        cq = dnconv_ref[3:4, :] * xe[CONV_CARRY:]
        for j in range(3):
            cq = cq + dnconv_ref[j:j + 1, :] * pltpu.roll(xe, 3 - j, 0)[CONV_CARRY:]
        qkv = _silu(cq)

        gt = gt_ref[bb]
        beta_all = jnp.where(live, _sigmoid(gt), 0.0)
        xg = gt + dtb_ref[...]
        softplus = jnp.maximum(xg, 0.0) + jnp.log1p(jnp.exp(-jnp.abs(xg)))
        g_all = jnp.where(live, -jnp.exp(alog_ref[...]) * softplus, 0.0)
        g_hi = g_all.astype(BF16)
        g_r1 = g_all - g_hi.astype(F32)
        g_mid = g_r1.astype(BF16)
        g_lo = (g_r1 - g_mid.astype(F32)).astype(BF16)

        for pi in range(MIX_PAIRS):
            ps = slice(pi * pr, (pi + 1) * pr)
            cums = _dot(sel, g_hi[ps]) + (_dot(sel, g_mid[ps]) + _dot(sel, g_lo[ps]))
            cdec = cums[0:pr]
            dlast = tuple(jnp.broadcast_to(cums[pr + 8 * ci:pr + 8 * ci + 1], (pr, LANES)) for ci in range(2))
            cdec_t = cdec.T
            for h in range(DN_HEADS):
                q = qkv[ps, h * dh:(h + 1) * dh]
                k = qkv[ps, DN_DIM + h * dh:DN_DIM + (h + 1) * dh]
                v = qkv[ps, 2 * DN_DIM + h * dh:2 * DN_DIM + (h + 1) * dh]
                q = q * lax.rsqrt(jnp.sum(q * q, -1, keepdims=True) + EPS) * (dh ** -0.5)
                k = k * lax.rsqrt(jnp.sum(k * k, -1, keepdims=True) + EPS)
                gl = DN_HEADS + h
                beta = beta_all[ps, h:h + 1]
                cd_col = cdec[:, gl:gl + 1]
                cd_row = cdec_t[gl:gl + 1, :]
                dmask = jnp.where(incl, jnp.exp(jnp.where(incl, cd_col - cd_row, 0.0)), 0.0)
                kb = k.astype(BF16)
                qk_kk = _dot_nt(jnp.concatenate([q.astype(BF16), kb], axis=0), kb)
                amat[bb, pi, h] = jnp.where(strict, beta * qk_kk[pr:2 * pr] * dmask, 0.0)
                e_cd = jnp.exp(cd_col)
                dl_own = jnp.where(first, dlast[0][:, gl:gl + 1], dlast[1][:, gl:gl + 1])
                local[bb, pi, h] = dict(
                    rhs=jnp.concatenate([v * beta, k * (beta * e_cd)], axis=1),
                    qk=(qk_kk[0:pr] * dmask).astype(BF16),
                    q_dec=(q * e_cd).astype(BF16),
                    k_dec_t=(k * jnp.exp(dl_own - cd_col)).T.astype(BF16),
                    g_last=[jnp.exp(d[:, gl:gl + 1]) for d in dlast])

    pairs = [(bb, pi, hp) for bb in range(nb) for pi in range(MIX_PAIRS) for hp in range(DN_HEADS // 2)]

    def pair_dot(xs, ys):
        prod = _dot(jnp.concatenate([x.astype(BF16) for x in xs], axis=1),
                    _blockdiag(*[y.astype(BF16) for y in ys]))
        return prod[:, 0:pr], prod[:, pr:2 * pr]

    in_block = lambda n: (ii // n) == (jj // n)
    base = in_block(SOLVE_BASE)
    pw = {key: jnp.where(base, amat[key], 0.0) for key in keys}
    tm = {key: -pw[key] for key in keys}
    for _ in range(2):
        for bb, pi, hp in pairs:
            ka, kb_ = (bb, pi, 2 * hp), (bb, pi, 2 * hp + 1)
            pw[ka], pw[kb_] = pair_dot((pw[ka], pw[kb_]), (pw[ka], pw[kb_]))
        for bb, pi, hp in pairs:
            ka, kb_ = (bb, pi, 2 * hp), (bb, pi, 2 * hp + 1)
            pa_, pb_ = pair_dot((tm[ka], tm[kb_]), (pw[ka], pw[kb_]))
            tm[ka] = tm[ka] + pw[ka] + pa_
            tm[kb_] = tm[kb_] + pw[kb_] + pb_
    n = SOLVE_BASE
    while n < c:
        sib = in_block(2 * n) & jnp.logical_not(in_block(n))
        lm = {key: jnp.where(sib, amat[key], 0.0) for key in keys}
        dl = {}
        for bb, pi, hp in pairs:
            ka, kb_ = (bb, pi, 2 * hp), (bb, pi, 2 * hp + 1)
            pa_, pb_ = pair_dot((tm[ka], tm[kb_]), (lm[ka], lm[kb_]))
            dl[ka], dl[kb_] = lm[ka] + pa_, lm[kb_] + pb_
        for bb, pi, hp in pairs:
            ka, kb_ = (bb, pi, 2 * hp), (bb, pi, 2 * hp + 1)
            pa_, pb_ = pair_dot((dl[ka], dl[kb_]), (tm[ka], tm[kb_]))
            tm[ka] = tm[ka] - (dl[ka] + pa_)
            tm[kb_] = tm[kb_] - (dl[kb_] + pb_)
        n *= 2
    for key in keys:
        rhs = local[key]["rhs"]
        sol = rhs + _dot(tm[key].astype(BF16), rhs.astype(BF16))
        local[key]["u"] = sol[:, 0:dh]
        local[key]["w"] = sol[:, dh:2 * dh].astype(BF16)

    state = {(bb, h): s_ref[bb, h] for bb in range(nb) for h in range(DN_HEADS)}
    outs = {(bb, h): [] for bb in range(nb) for h in range(DN_HEADS)}
    zeros = jnp.zeros((c, dh), BF16)
    hpairs = [(bb, hp) for bb in range(nb) for hp in range(DN_HEADS // 2)]
    for pi in range(MIX_PAIRS):
        for ci in range(2):
            rs = slice(ci * c, (ci + 1) * c)
            ws_qs = {}
            for bb, hp in hpairs:
                lhs = jnp.concatenate(
                    [jnp.concatenate([local[bb, pi, h]["w"][rs], local[bb, pi, h]["q_dec"][rs]], axis=0)
                     for h in (2 * hp, 2 * hp + 1)], axis=1)
                ws_qs[bb, hp] = _dot(lhs, _blockdiag(state[bb, 2 * hp].astype(BF16),
                                                     state[bb, 2 * hp + 1].astype(BF16)))
            v_full = {}
            for bb in range(nb):
                for h in range(DN_HEADS):
                    ws = ws_qs[bb, h // 2][0:c, (h % 2) * dh:(h % 2 + 1) * dh]
                    vb = (local[bb, pi, h]["u"][rs] - ws).astype(BF16)
                    v_full[bb, h] = jnp.concatenate([vb, zeros] if ci == 0 else [zeros, vb], axis=0)
            for bb, hp in hpairs:
                ha, hb_ = 2 * hp, 2 * hp + 1
                upd = _dot(jnp.concatenate([local[bb, pi, ha]["k_dec_t"], local[bb, pi, hb_]["k_dec_t"]], axis=1),
                           _blockdiag(v_full[bb, ha], v_full[bb, hb_]))
                state[bb, ha] = state[bb, ha] * local[bb, pi, ha]["g_last"][ci] + upd[:, 0:dh]
                state[bb, hb_] = state[bb, hb_] * local[bb, pi, hb_]["g_last"][ci] + upd[:, dh:2 * dh]
            for bb, hp in hpairs:
                ha, hb_ = 2 * hp, 2 * hp + 1
                intra = _dot(jnp.concatenate([local[bb, pi, ha]["qk"][rs], local[bb, pi, hb_]["qk"][rs]], axis=1),
                             _blockdiag(v_full[bb, ha], v_full[bb, hb_]))
                outs[bb, ha].append(ws_qs[bb, hp][c:2 * c, 0:dh] + intra[:, 0:dh])
                outs[bb, hb_].append(ws_qs[bb, hp][c:2 * c, dh:2 * dh] + intra[:, dh:2 * dh])

    z0 = 3 * D_CONV + 3 * DN_DIM
    for bb in range(nb):
        for h in range(DN_HEADS):
            s_ref[bb, h] = state[bb, h]
            o = jnp.concatenate(outs[bb, h], axis=0)
            z = p_ref[bb, :, z0 + h * dh:z0 + (h + 1) * dh].astype(F32)
            y = _rms(o, dnw_ref[...]) * _silu(z)
            y_ref[bb * r:(bb + 1) * r, D_CONV + h * dh:D_CONV + (h + 1) * dh] = y.astype(BF16)

    mix = _dot(y_ref[...], wo_ref[...])
    for bb in range(nb):
        res = _tile_rows((hb0_ref, hb1_ref, hb2_ref), head_ref, bb, t, shift)
        o_ref[bb] = res + mix[bb * r:(bb + 1) * r]


def _decay_selectors():
    r, c = MIX_PAIR, DN_CHUNK
    i = jnp.arange(r)[:, None]
    j = jnp.arange(r)[None, :]
    tri = ((i // c) == (j // c)) & (i >= j)
    last0 = jnp.broadcast_to(j < c, (8, r))
    last1 = jnp.broadcast_to(j >= c, (8, r))
    return jnp.concatenate([tri, last0, last1], axis=0).astype(BF16)


def _mixer(p, gt, src, head, shift, conv_a_w, dn_conv_w, a_log, dt_bias, dn_norm_w, w_out):
    b, lp, _ = p.shape
    d = src.shape[-1]
    r = MIX_ROWS
    nb = MIX_BATCH
    pad_lanes = lambda vec: jnp.zeros((1, LANES), F32).at[0, DN_HEADS:2 * DN_HEADS].set(vec.astype(F32))
    const = lambda shape: pl.BlockSpec(shape, lambda bi, t: (0,) * len(shape))
    prev_rows = lambda t: jnp.maximum(t * (r // CONV_CARRY) - 1, 0)
    qkv_cols = 3 * DN_DIM
    return pl.pallas_call(
        functools.partial(_mixer_kernel, shift=shift),
        grid=(b // nb, lp // r),
        in_specs=[pl.BlockSpec((nb, r, P_MAIN), lambda bi, t: (bi, t, 0)),
                  pl.BlockSpec((nb, CONV_CARRY, qkv_cols), lambda bi, t: (bi, prev_rows(t), 0)),
                  pl.BlockSpec((nb, CONV_CARRY, qkv_cols), lambda bi, t: (bi, prev_rows(t), 1)),
                  pl.BlockSpec((nb, r, LANES), lambda bi, t: (bi, t, 0))]
                 + _tile_specs(nb, d, shift) + [
                  const((ROW0, d)),
                  const((MIX_PAIR + 16, MIX_PAIR)),
                  const((3, D_CONV)),
                  const((4, 3 * DN_DIM)),
                  const((1, LANES)),
                  const((1, LANES)),
                  const((1, DN_HEAD_DIM)),
                  _resident((D_CONV + DN_DIM, d))],
        out_specs=pl.BlockSpec((nb, r, d), lambda bi, t: (bi, t, 0)),
        out_shape=jax.ShapeDtypeStruct((b, lp, d), F32),
        scratch_shapes=[pltpu.VMEM((nb * r, D_CONV + DN_DIM), BF16),
                        pltpu.VMEM((nb, DN_HEADS, DN_HEAD_DIM, DN_HEAD_DIM), F32)],
        compiler_params=_cparams(("arbitrary", "arbitrary")),
        name="gdn_mixer",
    )(p, p, p, gt, src, src, src, head, _decay_selectors(), conv_a_w.astype(F32), dn_conv_w.astype(F32),
      pad_lanes(a_log), pad_lanes(dt_bias), dn_norm_w.astype(F32).reshape(1, DN_HEAD_DIM),
      w_out.astype(BF16))


LOG2E = 1.4426950408889634
SWA_HALF = SWA_BLOCK // 2
SWA_KEYS = N_META + SWA_BLOCK + SWA_HALF


def _qkv_proj_kernel(x_ref, nw_ref, w_ref, hw_ref, o_ref):
    xn = _rms(x_ref[...], nw_ref[...]).astype(BF16)
    n_norm = hw_ref.shape[1]
    lo = lax.broadcasted_iota(jnp.int32, (1, LANES), 1) < SWA_HEAD_DIM
    for j in range(w_ref.shape[1] // MXU_COLS):
        yy = _dot(xn, w_ref[:, j * MXU_COLS:(j + 1) * MXU_COLS])
        for s in range(MXU_COLS // LANES):
            cs = slice(j * MXU_COLS + s * LANES, j * MXU_COLS + (s + 1) * LANES)
            y = yy[:, s * LANES:(s + 1) * LANES]
            if cs.start < n_norm:
                y2 = y * y
                ms_lo = jnp.sum(jnp.where(lo, y2, 0.0), -1, keepdims=True) * (1.0 / SWA_HEAD_DIM)
                ms_hi = jnp.sum(jnp.where(lo, 0.0, y2), -1, keepdims=True) * (1.0 / SWA_HEAD_DIM)
                y = y * jnp.where(lo, lax.rsqrt(ms_lo + EPS), lax.rsqrt(ms_hi + EPS)) * hw_ref[:, cs]
            o_ref[:, cs] = y.astype(o_ref.dtype)


def _qkv_proj(x, nw, w, hw, *, tm):
    m, k = x.shape
    n = w.shape[1]
    return pl.pallas_call(
        _qkv_proj_kernel,
        grid=(m // tm,),
        in_specs=[pl.BlockSpec((tm, k), lambda i: (i, 0)),
                  _resident((1, k)),
                  _resident((k, n)),
                  _resident((1, hw.shape[1]))],
        out_specs=pl.BlockSpec((tm, n), lambda i: (i, 0)),
        out_shape=jax.ShapeDtypeStruct((m, n), BF16),
        compiler_params=_cparams(("parallel",)),
        name="qkv_proj",
    )(x, nw, w, hw)


SWA_STEP_BLOCKS = 3


def _swa_kernel(sink_ref, q_ref, km_ref, kp_ref, kc_ref, vm_ref, vp_ref, vc_ref, h_ref, wo_ref, o_ref,
                att_ref):
    t = pl.program_id(1)
    blk = SWA_BLOCK
    hb = SWA_HALF
    nk = SWA_KEYS
    lo = lax.broadcasted_iota(jnp.int32, (1, LANES), 1) < SWA_HEAD_DIM

    def mask_bias(i, hf):
        r = hf * hb + lax.broadcasted_iota(jnp.int32, (hb, nk), 0)
        cidx = lax.broadcasted_iota(jnp.int32, (hb, nk), 1)
        n_prev = blk - hf * hb
        prev_j = cidx - N_META + hf * hb
        cur_j = cidx - N_META - n_prev
        meta_ok = (cidx < N_META) & ((i > 0) | (cidx <= r - PAD_ROWS))
        prev_ok = (cidx >= N_META) & (cidx < N_META + n_prev) & (prev_j > r) & (i >= 2)
        cur_ok = (cidx >= N_META + n_prev) & (cur_j <= r) & (i >= 1)
        bh = jnp.where(meta_ok | prev_ok | cur_ok, 0.0, NEG)
        return jnp.concatenate([bh] * SWA_GROUP, axis=0)

    def slab_rows(m_ref, p_ref, c_ref, slab):
        cs = slice(slab * LANES, (slab + 1) * LANES)
        full = jnp.concatenate([m_ref[0, :, cs], p_ref[0, :, cs], c_ref[0, :, cs]], axis=0)
        swapped = pltpu.roll(full, SWA_HEAD_DIM, 1)
        zero = jnp.zeros_like(full)
        even = (jnp.where(lo, full, zero), jnp.where(lo, zero, swapped))
        odd = (jnp.where(lo, swapped, zero), jnp.where(lo, zero, full))
        return even, odd

    def window(x, j, hf):
        a = N_META + j * blk + hf * hb
        return jnp.concatenate([x[0:N_META], x[a:a + nk - N_META]], axis=0)

    kslabs = [slab_rows(km_ref, kp_ref, kc_ref, s) for s in range(SWA_KV_HEADS // 2)]
    vslabs = [slab_rows(vm_ref, vp_ref, vc_ref, s) for s in range(SWA_KV_HEADS // 2)]
    bias = {(j, hf): mask_bias(t * SWA_STEP_BLOCKS + j, hf)
            for j in range(SWA_STEP_BLOCKS) for hf in range(2)}
    problems = [(j, kv, hf) for j in range(SWA_STEP_BLOCKS) for kv in range(SWA_KV_HEADS) for hf in range(2)]

    def logits(j, kv, hf):
        rows = slice(j * blk + hf * hb, j * blk + (hf + 1) * hb)
        base = kv * SWA_GROUP * SWA_HEAD_DIM
        lhs = jnp.concatenate([q_ref[0, rows, base:base + LANES],
                               q_ref[0, rows, base + LANES:base + 2 * LANES]], axis=0)
        k_lo, k_hi = kslabs[kv // 2][kv % 2]
        return jnp.concatenate([_dot_nt(lhs, window(k_lo, j, hf)), _dot_nt(lhs, window(k_hi, j, hf))],
                               axis=0) + bias[j, hf]

    grp = lax.broadcasted_iota(jnp.int32, (SWA_GROUP * hb, 1), 0) // hb
    sinks = []
    for kv in range(SWA_KV_HEADS):
        sink = jnp.zeros((SWA_GROUP * hb, 1), F32)
        for gi, g in enumerate((0, 2, 1, 3)):
            sink = jnp.where(grp == gi, sink_ref[kv * SWA_GROUP + g], sink)
        sinks.append(sink)

    def row_max(st, j, kv, hf):
        st["m"] = jnp.maximum(jnp.max(st["lg"], axis=-1, keepdims=True), sinks[kv])

    def exp_sum(st, j, kv, hf):
        e = jnp.exp2(st.pop("lg") - st["m"])
        st["den"] = jnp.sum(e, axis=-1, keepdims=True) + jnp.exp2(sinks[kv] - st.pop("m"))
        st["p"] = e.astype(BF16)

    def weighted_sum(st, j, kv, hf):
        p = st.pop("p")
        v_lo, v_hi = vslabs[kv // 2][kv % 2]
        st["o"] = (_dot(p[0:2 * hb], window(v_lo, j, hf))
                   + _dot(p[2 * hb:4 * hb], window(v_hi, j, hf)))
        st["rden"] = 1.0 / st.pop("den")

    def normalise(st, j, kv, hf):
        rden = st.pop("rden")
        o = st.pop("o") * jnp.where(lo, rden[0:2 * hb], rden[2 * hb:4 * hb])
        rows = slice(j * blk + hf * hb, j * blk + (hf + 1) * hb)
        base = kv * SWA_GROUP * SWA_HEAD_DIM
        att_ref[rows, base:base + LANES] = o[0:hb].astype(BF16)
        att_ref[rows, base + LANES:base + 2 * LANES] = o[hb:2 * hb].astype(BF16)

    stages = (None, row_max, exp_sum, weighted_sum, normalise)
    states = [dict() for _ in problems]
    for step in range(len(problems) + len(stages) - 1):
        for depth, stage in enumerate(stages):
            n = step - depth
            if 0 <= n < len(problems):
                if stage is None:
                    states[n]["lg"] = logits(*problems[n])
                else:
                    stage(states[n], *problems[n])

    o_ref[0] = h_ref[0] + _dot(att_ref[...], wo_ref[...])


def _swa(qkv, sinks, h, wo):
    b, lp, _ = qkv.shape
    d = h.shape[-1]
    blk = SWA_BLOCK
    rows = SWA_STEP_BLOCKS * blk
    nq = SWA_HEADS * SWA_HEAD_DIM
    nkv = SWA_KV_HEADS * SWA_HEAD_DIM
    kcol = nq // nkv
    vcol = kcol + 1
    meta_blk = PAD_ROWS // N_META
    prev = lambda t: jnp.maximum(t * SWA_STEP_BLOCKS - 1, 0)
    return pl.pallas_call(
        _swa_kernel,
        grid=(b, lp // rows),
        in_specs=[pl.BlockSpec(memory_space=pltpu.SMEM),
                  pl.BlockSpec((1, rows, nq), lambda bi, t: (bi, t, 0)),
                  pl.BlockSpec((1, N_META, nkv), lambda bi, t: (bi, meta_blk, kcol)),
                  pl.BlockSpec((1, blk, nkv), lambda bi, t: (bi, prev(t), kcol)),
                  pl.BlockSpec((1, rows, nkv), lambda bi, t: (bi, t, kcol)),
                  pl.BlockSpec((1, N_META, nkv), lambda bi, t: (bi, meta_blk, vcol)),
                  pl.BlockSpec((1, blk, nkv), lambda bi, t: (bi, prev(t), vcol)),
                  pl.BlockSpec((1, rows, nkv), lambda bi, t: (bi, t, vcol)),
                  pl.BlockSpec((1, rows, d), lambda bi, t: (bi, t, 0)),
                  _resident((nq, d))],
        out_specs=pl.BlockSpec((1, rows, d), lambda bi, t: (bi, t, 0)),
        out_shape=jax.ShapeDtypeStruct((b, lp, d), F32),
        scratch_shapes=[pltpu.VMEM((rows, nq), BF16)],
        compiler_params=_cparams(("parallel", "arbitrary")),
        name="swa",
    )((sinks.astype(F32) * LOG2E), qkv, qkv, qkv, qkv, qkv, qkv, qkv, h, wo.astype(BF16))


def kernel(x, meta_tokens, attn_norm_w, ffn_norm_w, mix_w_in, conv_a_w, dn_conv_w, dn_a_log,
           dn_dt_bias, dn_norm_w, mix_w_out, swa_wq, swa_wk, swa_wv, swa_q_norm_w, swa_k_norm_w,
           swa_sinks, swa_wo, ffn_w_up, ffn_conv_w, ffn_w_down):
    b, seq, d = x.shape
    lp = ROW0 + seq
    head = jnp.concatenate([jnp.zeros((PAD_ROWS, d), x.dtype), meta_tokens.astype(x.dtype)], axis=0)
    h = None
    depth = attn_norm_w.shape[0]
    for layer in range(depth):
        i = layer // 2
        nw = attn_norm_w[layer].reshape(1, d)
        src, shift = (x, 1) if h is None else (h, 0)
        if layer % 2 == 0:
            w_in = mix_w_in[i]
            wg = jnp.zeros((d, LANES), BF16).at[:, :2 * DN_HEADS].set(w_in[:, P_MAIN:].astype(BF16))
            p, gt = _in_proj(src, head, shift, nw, w_in[:, :P_MAIN].astype(BF16), wg)
            h = _mixer(p, gt, src, head, shift, conv_a_w[i], dn_conv_w[i],
                       dn_a_log[i], dn_dt_bias[i], dn_norm_w[i], mix_w_out[i])
        else:
            wqkv = jnp.concatenate([swa_wq[i], swa_wk[i], swa_wv[i]], axis=1).astype(BF16)
            head_w = jnp.concatenate([
                jnp.tile(swa_q_norm_w[i].astype(F32) * (SWA_HEAD_DIM ** -0.5 * LOG2E), SWA_HEADS),
                jnp.tile(swa_k_norm_w[i].astype(F32), SWA_KV_HEADS)]).reshape(1, -1)
            qkv = _qkv_proj(h.reshape(b * lp, d), nw, wqkv, head_w, tm=1024)
            h = _swa(qkv.reshape(b, lp, -1), swa_sinks[i], h, swa_wo[i])
        last = layer == depth - 1
        h = _ffn(h, ffn_norm_w[layer], ffn_w_up[layer], ffn_conv_w[layer], ffn_w_down[layer],
                 tm=1024 if last else 704, first_row=ROW0 if last else 0)
    return h
```
